```python
import math
import jax, jax.numpy as jnp
from jax import lax
import numpy as np

D_MODEL = 1024
BATCH = 8
SEQ = 4096
DEPTH = 1

SB_HEADS = 8
SB_HEAD_DIM = 64
SB_WIDTH = SB_HEADS * SB_HEAD_DIM
MLA_HEADS = 8
MLA_NOPE_DIM = 64
MLA_ROPE_DIM = 32
MLA_V_DIM = 64
MLA_Q_RANK = 384
MLA_KV_RANK = 256
MLA_QK_DIM = MLA_NOPE_DIM + MLA_ROPE_DIM
MLA_WIDTH = MLA_HEADS * MLA_V_DIM
ROPE_THETA = 10000.0
Q_BLOCK = 128
N_GROUPS = 4
EXPERTS_PER_GROUP = 8
N_EXPERTS = N_GROUPS * EXPERTS_PER_GROUP
TOP_K_IN_GROUP = 2
D_EXPERT = 256
DEEPNORM_ALPHA = (2.0 * DEPTH) ** 0.25
DEEPNORM_BETA = (8.0 * DEPTH) ** -0.25
LN_EPS = 1e-5
RMS_EPS = 1e-6
IN_SPLITS = (SB_WIDTH, SB_WIDTH, SB_WIDTH, MLA_Q_RANK, MLA_KV_RANK, MLA_ROPE_DIM, D_MODEL, D_MODEL)
IN_COLS = sum(IN_SPLITS)
N_MOD = 6

kernel_name = "hybrid_sb_mla_hmoe_deepnorm_adaln"


def layer_norm(x, g, b):
    xf = x.astype(jnp.float32)
    mu = jnp.mean(xf, axis=-1, keepdims=True)
    var = jnp.mean(jnp.square(xf - mu), axis=-1, keepdims=True)
    y = (xf - mu) * lax.rsqrt(var + LN_EPS)
    return (y * g.astype(jnp.float32) + b.astype(jnp.float32)).astype(x.dtype)


def rms_norm(x, g):
    xf = x.astype(jnp.float32)
    y = xf * lax.rsqrt(jnp.mean(jnp.square(xf), axis=-1, keepdims=True) + RMS_EPS)
    return (y * g.astype(jnp.float32)).astype(x.dtype)


def rope_tables(positions):
    inv_freq = 1.0 / (ROPE_THETA ** (jnp.arange(0, MLA_ROPE_DIM, 2, dtype=jnp.float32) / MLA_ROPE_DIM))
    ang = positions.astype(jnp.float32)[..., None] * inv_freq
    return jnp.cos(ang), jnp.sin(ang)


def apply_rope(x, cos, sin):
    half = x.shape[-1] // 2
    xf = x.astype(jnp.float32)
    x1, x2 = xf[..., :half], xf[..., half:]
    cs, sn = cos[:, :, None, :], sin[:, :, None, :]
    return jnp.concatenate([x1 * cs - x2 * sn, x1 * sn + x2 * cs], axis=-1).astype(x.dtype)


def stick_breaking_attention(q, k, v):
    B, S, H, d = q.shape
    nb = S // Q_BLOCK
    scale = 1.0 / math.sqrt(d)
    q_blocks = q.reshape(B, nb, Q_BLOCK, H, d).transpose(1, 0, 2, 3, 4)
    key_idx = jnp.arange(S)

    def block(args):
        q_blk, start = args
        z = jnp.einsum('bqhd,bkhd->bhqk', q_blk, k, preferred_element_type=jnp.float32) * scale
        q_idx = start + jnp.arange(Q_BLOCK)
        mask = key_idx[None, :] < q_idx[:, None]
        log_1m_beta = jnp.where(mask, jax.nn.log_sigmoid(-z), 0.0)
        suffix = lax.cumsum(log_1m_beta, axis=3, reverse=True) - log_1m_beta
        w = jnp.where(mask, jnp.exp(jax.nn.log_sigmoid(z) + suffix), 0.0)
        return jnp.einsum('bhqk,bkhd->bqhd', w.astype(v.dtype), v)

    out = lax.map(block, (q_blocks, jnp.arange(nb) * Q_BLOCK))
    return out.transpose(1, 0, 2, 3, 4).reshape(B, S, H, d)


def mla_attention(q_nope, q_rope, k_nope, k_rope, v):
    B, S, H, dn = q_nope.shape
    dr = q_rope.shape[-1]
    nb = S // Q_BLOCK
    scale = 1.0 / math.sqrt(MLA_QK_DIM)
    qn_blocks = q_nope.reshape(B, nb, Q_BLOCK, H, dn).transpose(1, 0, 2, 3, 4)
    qr_blocks = q_rope.reshape(B, nb, Q_BLOCK, H, dr).transpose(1, 0, 2, 3, 4)
    key_idx = jnp.arange(S)

    def block(args):
        qn, qr, start = args
        s = (jnp.einsum('bqhd,bkhd->bhqk', qn, k_nope, preferred_element_type=jnp.float32)
             + jnp.einsum('bqhr,bkr->bhqk', qr, k_rope, preferred_element_type=jnp.float32)) * scale
        q_idx = start + jnp.arange(Q_BLOCK)
        mask = key_idx[None, :] <= q_idx[:, None]
        p = jax.nn.softmax(jnp.where(mask, s, -jnp.inf), axis=-1)
        return jnp.einsum('bhqk,bkhd->bqhd', p.astype(v.dtype), v)

    out = lax.map(block, (qn_blocks, qr_blocks, jnp.arange(nb) * Q_BLOCK))
    return out.transpose(1, 0, 2, 3, 4).reshape(B, S, H, v.shape[-1])


def hierarchical_moe(t, w_rg, b_rg, w_re, b_re, w_g, w_u, w_d):
    N = t.shape[0]
    group_logits = (t @ w_rg + b_rg).astype(jnp.float32)
    group_probs = jax.nn.softmax(group_logits, axis=-1)
    p_group, g_idx = lax.top_k(group_probs, 1)
    expert_logits = (t @ w_re + b_re).astype(jnp.float32).reshape(N, N_GROUPS, EXPERTS_PER_GROUP)
    sel_logits = jnp.take_along_axis(expert_logits, g_idx[:, :, None], axis=1)[:, 0]
    top_vals, top_idx = lax.top_k(sel_logits, TOP_K_IN_GROUP)
    weights = jax.nn.softmax(top_vals, axis=-1) * p_group
    expert_id = g_idx * EXPERTS_PER_GROUP + top_idx
    combine = jnp.sum(jax.nn.one_hot(expert_id, N_EXPERTS, dtype=jnp.float32) * weights[..., None],
                      axis=1).astype(t.dtype)
    out = jnp.zeros_like(t)
    for gi in range(N_GROUPS):
        sl = slice(gi * EXPERTS_PER_GROUP, (gi + 1) * EXPERTS_PER_GROUP)
        h = jax.nn.silu(jnp.einsum('nd,edf->nef', t, w_g[sl])) * jnp.einsum('nd,edf->nef', t, w_u[sl])
        h = h * combine[:, sl, None]
        out = out + jnp.einsum('nef,efd->nd', h, w_d[sl])
    return out


def setup_inputs(seed: int = 0) -> dict:
    key = jax.random.key(seed)
    ks = jax.random.split(key, 26)
    f32 = jnp.float32
    D = D_MODEL

    def nrm(k, shape, scale):
        return jax.random.normal(k, shape, f32) * scale

    x = jax.random.normal(ks[0], (BATCH, SEQ, D), f32)
    c = jax.random.normal(ks[1], (BATCH, D), f32)
    offset = jax.random.randint(ks[2], (BATCH,), 0, 1024, dtype=jnp.int32)
    positions = offset[:, None] + jnp.arange(SEQ, dtype=jnp.int32)[None, :]

    col_scale = np.ones((IN_COLS,), np.float32)
    col_scale[2 * SB_WIDTH:3 * SB_WIDTH] = DEEPNORM_BETA
    w_in = nrm(ks[5], (DEPTH, D, IN_COLS), D ** -0.5) * jnp.asarray(col_scale)

    return {
        "x": x,
        "c": c,
        "positions": positions,
        "w_ada": nrm(ks[3], (DEPTH, D, N_MOD * D), D ** -0.5),
        "b_ada": nrm(ks[4], (DEPTH, N_MOD * D), 0.02),
        "w_in": w_in,
        "mla_q_norm_g": 1.0 + nrm(ks[6], (DEPTH, MLA_Q_RANK), 0.02),
        "w_q_up": nrm(ks[7], (DEPTH, MLA_Q_RANK, MLA_HEADS * MLA_QK_DIM), MLA_Q_RANK ** -0.5),
        "mla_kv_norm_g": 1.0 + nrm(ks[8], (DEPTH, MLA_KV_RANK), 0.02),
        "w_kv_up": nrm(ks[9], (DEPTH, MLA_KV_RANK, MLA_HEADS * (MLA_NOPE_DIM + MLA_V_DIM)), MLA_KV_RANK ** -0.5),
        "w_branch_sb": nrm(ks[10], (DEPTH, SB_WIDTH, D), SB_WIDTH ** -0.5 * DEEPNORM_BETA),
        "w_branch_mla": nrm(ks[11], (DEPTH, MLA_WIDTH, D), MLA_WIDTH ** -0.5 * DEEPNORM_BETA),
        "w_out": nrm(ks[12], (DEPTH, D, D), D ** -0.5 * DEEPNORM_BETA),
        "ln1_g": 1.0 + nrm(ks[13], (DEPTH, D), 0.02),
        "ln1_b": nrm(ks[14], (DEPTH, D), 0.02),
        "w_router_group": nrm(ks[15], (DEPTH, D, N_GROUPS), D ** -0.5),
        "b_router_group": nrm(ks[16], (DEPTH, N_GROUPS), 0.01),
        "w_router_expert": nrm(ks[17], (DEPTH, D, N_EXPERTS), D ** -0.5),
        "b_router_expert": nrm(ks[18], (DEPTH, N_EXPERTS), 0.01),
        "w_exp_gate": nrm(ks[19], (DEPTH, N_EXPERTS, D, D_EXPERT), D ** -0.5),
        "w_exp_up": nrm(ks[20], (DEPTH, N_EXPERTS, D, D_EXPERT), D ** -0.5 * DEEPNORM_BETA),
        "w_exp_down": nrm(ks[21], (DEPTH, N_EXPERTS, D_EXPERT, D), D_EXPERT ** -0.5 * DEEPNORM_BETA),
        "ln2_g": 1.0 + nrm(ks[22], (DEPTH, D), 0.02),
        "ln2_b": nrm(ks[23], (DEPTH, D), 0.02),
    }


def reference(x, c, positions, w_ada, b_ada, w_in, mla_q_norm_g, w_q_up, mla_kv_norm_g, w_kv_up,
              w_branch_sb, w_branch_mla, w_out, ln1_g, ln1_b, w_router_group, b_router_group,
              w_router_expert, b_router_expert, w_exp_gate, w_exp_up, w_exp_down, ln2_g, ln2_b):
    B, S, D = x.shape
    cos, sin = rope_tables(positions)
    c_act = jax.nn.silu(c)
    split_points = []
    acc = 0
    for width in IN_SPLITS[:-1]:
        acc += width
        split_points.append(acc)

    for l in range(DEPTH):
        mod = c_act @ w_ada[l] + b_ada[l]
        shift1, scale1, gate1, shift2, scale2, gate2 = [m[:, None, :] for m in jnp.split(mod, N_MOD, axis=-1)]

        u = x * (1.0 + scale1) + shift1
        proj = u @ w_in[l]
        q_sb, k_sb, v_sb, q_down, kv_down, k_rope_raw, g_sb, g_mla = jnp.split(proj, split_points, axis=-1)

        o_sb = stick_breaking_attention(q_sb.reshape(B, S, SB_HEADS, SB_HEAD_DIM),
                                        k_sb.reshape(B, S, SB_HEADS, SB_HEAD_DIM),
                                        v_sb.reshape(B, S, SB_HEADS, SB_HEAD_DIM))

        q = (rms_norm(q_down, mla_q_norm_g[l]) @ w_q_up[l]).reshape(B, S, MLA_HEADS, MLA_QK_DIM)
        q_nope = q[..., :MLA_NOPE_DIM]
        q_rope = apply_rope(q[..., MLA_NOPE_DIM:], cos, sin)
        kv = (rms_norm(kv_down, mla_kv_norm_g[l]) @ w_kv_up[l]).reshape(B, S, MLA_HEADS, MLA_NOPE_DIM + MLA_V_DIM)
        k_nope = kv[..., :MLA_NOPE_DIM]
        v_mla = kv[..., MLA_NOPE_DIM:]
        k_rope = apply_rope(k_rope_raw[:, :, None, :], cos, sin)[:, :, 0, :]
        o_mla = mla_attention(q_nope, q_rope, k_nope, k_rope, v_mla)

        y_sb = o_sb.reshape(B, S, SB_WIDTH) @ w_branch_sb[l]
        y_mla = o_mla.reshape(B, S, MLA_WIDTH) @ w_branch_mla[l]
        mixed = jax.nn.sigmoid(g_sb) * y_sb + jax.nn.sigmoid(g_mla) * y_mla
        attn_out = mixed @ w_out[l]
        x = layer_norm(DEEPNORM_ALPHA * x + gate1 * attn_out, ln1_g[l], ln1_b[l])

        u2 = x * (1.0 + scale2) + shift2
        moe = hierarchical_moe(u2.reshape(B * S, D), w_router_group[l], b_router_group[l],
                               w_router_expert[l], b_router_expert[l],
                               w_exp_gate[l], w_exp_up[l], w_exp_down[l]).reshape(B, S, D)
        x = layer_norm(DEEPNORM_ALPHA * x + gate2 * moe, ln2_g[l], ln2_b[l])
    return x
```

```python
import functools
import math

import jax
import jax.numpy as jnp
from jax import lax
from jax.experimental import pallas as pl
from jax.experimental.pallas import tpu as pltpu

F32 = jnp.float32
BF16 = jnp.bfloat16

D_MODEL = 1024
SB_HEADS = 8
SB_HEAD_DIM = 64
SB_WIDTH = SB_HEADS * SB_HEAD_DIM
MLA_HEADS = 8
MLA_NOPE_DIM = 64
MLA_ROPE_DIM = 32
MLA_V_DIM = 64
MLA_Q_RANK = 384
MLA_KV_RANK = 256
MLA_QK_DIM = MLA_NOPE_DIM + MLA_ROPE_DIM
MLA_WIDTH = MLA_HEADS * MLA_V_DIM
ROPE_THETA = 10000.0
N_GROUPS = 4
EXPERTS_PER_GROUP = 8
N_EXPERTS = N_GROUPS * EXPERTS_PER_GROUP
D_EXPERT = 256
DEPTH = 1
DEEPNORM_ALPHA = (2.0 * DEPTH) ** 0.25
LN_EPS = 1e-5
RMS_EPS = 1e-6
N_MOD = 6

LANES = 128
MLA_HEAD_LANES = 128
VMEM_LIMIT_BYTES = 56 * 1024 * 1024

PROJ_ROWS = 256
ATTN_Q = 256
ATTN_K = 256
MERGE_ROWS = 256
MOE_ROWS = 1024


def _cparams(sem):
    return pltpu.CompilerParams(dimension_semantics=sem, vmem_limit_bytes=VMEM_LIMIT_BYTES)


def _dot(a, b):
    return jnp.dot(a, b, preferred_element_type=F32)


def _dot_t(a, b):
    return lax.dot_general(a, b, (((1,), (1,)), ((), ())), preferred_element_type=F32)


def _adaln_kernel(c_ref, w_ref, b_ref, o_ref):
    c = c_ref[...]
    ca = c * jax.nn.sigmoid(c)
    o_ref[...] = _dot(ca.astype(BF16), w_ref[...].astype(BF16)) + b_ref[...]


def _adaln_mod(c, w_ada, b_ada):
    bsz, d = c.shape
    return pl.pallas_call(
        _adaln_kernel,
        grid=(N_MOD,),
        in_specs=[pl.BlockSpec((bsz, d), lambda j: (0, 0)),
                  pl.BlockSpec((d, d), lambda j: (0, j)),
                  pl.BlockSpec((1, d), lambda j: (0, j))],
        out_specs=pl.BlockSpec((bsz, d), lambda j: (0, j)),
        out_shape=jax.ShapeDtypeStruct((bsz, N_MOD * d), F32),
        compiler_params=_cparams(("arbitrary",)),
        name="adaln_mod",
    )(c, w_ada, b_ada.reshape(1, N_MOD * d))


def _in_proj_kernel(x_ref, scale_ref, shift_ref, pos_ref, invf_ref,
                    w_sb_ref, w_dn_ref, w_kr_ref, w_g_ref, qg_ref, kvg_ref,
                    w_qu_ref, w_ku_ref, w_vu_ref,
                    sb_ref, qm_ref, km_ref, vm_ref, g_ref):
    u = (x_ref[...] * (1.0 + scale_ref[...]) + shift_ref[...]).astype(BF16)

    sb = _dot(u, w_sb_ref[...])
    sb_scale = 1.0 / math.sqrt(SB_HEAD_DIM)
    sb_ref[:, :SB_WIDTH] = (sb[:, :SB_WIDTH] * sb_scale).astype(BF16)
    sb_ref[:, SB_WIDTH:] = sb[:, SB_WIDTH:].astype(BF16)

    g_ref[...] = _dot(u, w_g_ref[...])

    ang = pos_ref[...] * invf_ref[...]
    cos = jnp.cos(ang)
    sin = jnp.sin(ang)

    dn = _dot(u, w_dn_ref[...])
    q_dn = dn[:, :MLA_Q_RANK]
    kv_dn = dn[:, MLA_Q_RANK:]
    qn = q_dn * lax.rsqrt(jnp.mean(q_dn * q_dn, axis=-1, keepdims=True) + RMS_EPS) * qg_ref[...]
    kvn = kv_dn * lax.rsqrt(jnp.mean(kv_dn * kv_dn, axis=-1, keepdims=True) + RMS_EPS) * kvg_ref[...]
    qn = qn.astype(BF16)
    kvn = kvn.astype(BF16)

    width = MLA_HEADS * MLA_HEAD_LANES
    qu = _dot(qn, w_qu_ref[...])
    ku = _dot(kvn, w_ku_ref[...])
    vm_ref[...] = _dot(kvn, w_vu_ref[...]).astype(BF16)
    kr = _dot(u, w_kr_ref[...])
    kr = kr[:, :LANES] * cos + kr[:, LANES:] * sin
    mla_scale = 1.0 / math.sqrt(MLA_QK_DIM)
    for h in range(MLA_HEADS):
        sl = slice(h * MLA_HEAD_LANES, (h + 1) * MLA_HEAD_LANES)
        sl_rot = slice(width + h * MLA_HEAD_LANES, width + (h + 1) * MLA_HEAD_LANES)
        qm_ref[:, sl] = ((qu[:, sl] * cos + qu[:, sl_rot] * sin) * mla_scale).astype(BF16)
        km_ref[:, sl] = (ku[:, sl] + kr).astype(BF16)


def _rope_inv_freq_tile():
    inv_freq = 1.0 / (ROPE_THETA ** (jnp.arange(0, MLA_ROPE_DIM, 2, dtype=F32) / MLA_ROPE_DIM))
    tile = jnp.zeros((LANES,), F32)
    half = MLA_ROPE_DIM // 2
    tile = tile.at[MLA_NOPE_DIM:MLA_NOPE_DIM + half].set(inv_freq)
    tile = tile.at[MLA_NOPE_DIM + half:MLA_NOPE_DIM + 2 * half].set(inv_freq)
    return tile.reshape(1, LANES)


def _rotate_half_cols(w):
    half = MLA_ROPE_DIM // 2
    return jnp.concatenate([-w[..., half:], w[..., :half]], axis=-1)


def _prep_in_weights(w_in, w_q_up, w_kv_up):
    d = w_in.shape[0]
    o = 0
    w_sb = w_in[:, o:o + 3 * SB_WIDTH]; o += 3 * SB_WIDTH
    w_dn = w_in[:, o:o + MLA_Q_RANK + MLA_KV_RANK]; o += MLA_Q_RANK + MLA_KV_RANK
    w_kr = w_in[:, o:o + MLA_ROPE_DIM]; o += MLA_ROPE_DIM
    w_g = w_in[:, o:]

    def rope_tile(w):
        z = jnp.zeros((w.shape[0], LANES), w.dtype)
        return z.at[:, MLA_NOPE_DIM:MLA_NOPE_DIM + MLA_ROPE_DIM].set(w)

    w_kr2 = jnp.concatenate([rope_tile(w_kr), rope_tile(_rotate_half_cols(w_kr))], axis=1)

    wq = w_q_up.reshape(MLA_Q_RANK, MLA_HEADS, MLA_QK_DIM)
    q_base = jnp.zeros((MLA_Q_RANK, MLA_HEADS, MLA_HEAD_LANES), w_q_up.dtype).at[:, :, :MLA_QK_DIM].set(wq)
    q_rot = jnp.zeros((MLA_Q_RANK, MLA_HEADS, MLA_HEAD_LANES), w_q_up.dtype)
    q_rot = q_rot.at[:, :, MLA_NOPE_DIM:MLA_QK_DIM].set(_rotate_half_cols(wq[:, :, MLA_NOPE_DIM:]))
    w_qu = jnp.concatenate([q_base.reshape(MLA_Q_RANK, -1), q_rot.reshape(MLA_Q_RANK, -1)], axis=1)

    wkv = w_kv_up.reshape(MLA_KV_RANK, MLA_HEADS, MLA_NOPE_DIM + MLA_V_DIM)
    w_ku = jnp.zeros((MLA_KV_RANK, MLA_HEADS, MLA_HEAD_LANES), w_kv_up.dtype)
    w_ku = w_ku.at[:, :, :MLA_NOPE_DIM].set(wkv[:, :, :MLA_NOPE_DIM]).reshape(MLA_KV_RANK, -1)
    w_vu = wkv[:, :, MLA_NOPE_DIM:].reshape(MLA_KV_RANK, MLA_WIDTH)
    del d
    return tuple(w.astype(BF16) for w in (w_sb, w_dn, w_kr2, w_g, w_qu, w_ku, w_vu))


def _in_proj(x2, mod4, posf, invf, weights, qg, kvg, seq):
    n, d = x2.shape
    tm = min(PROJ_ROWS, seq)
    tiles_per_seq = seq // tm
    w_sb, w_dn, w_kr2, w_g, w_qu, w_ku, w_vu = weights

    def whole(a):
        return pl.BlockSpec(a.shape, lambda i: (0,) * a.ndim)

    def mod_spec(k):
        return pl.BlockSpec((None, None, 1, d), lambda i: (i // tiles_per_seq, k, 0, 0))

    def rows(width):
        return pl.BlockSpec((tm, width), lambda i: (i, 0))

    width = MLA_HEADS * MLA_HEAD_LANES
    return pl.pallas_call(
        _in_proj_kernel,
        grid=(n // tm,),
        in_specs=[rows(d), mod_spec(1), mod_spec(0), rows(1), whole(invf),
                  whole(w_sb), whole(w_dn), whole(w_kr2), whole(w_g), whole(qg), whole(kvg),
                  whole(w_qu), whole(w_ku), whole(w_vu)],
        out_specs=[rows(3 * SB_WIDTH), rows(width), rows(width), rows(MLA_WIDTH), rows(2 * d)],
        out_shape=[jax.ShapeDtypeStruct((n, 3 * SB_WIDTH), BF16),
                   jax.ShapeDtypeStruct((n, width), BF16),
                   jax.ShapeDtypeStruct((n, width), BF16),
                   jax.ShapeDtypeStruct((n, MLA_WIDTH), BF16),
                   jax.ShapeDtypeStruct((n, 2 * d), F32)],
        compiler_params=_cparams(("arbitrary",)),
        name="in_proj",
    )(x2, mod4, mod4, posf, invf, w_sb, w_dn, w_kr2, w_g, qg, kvg, w_qu, w_ku, w_vu)


def _suffix_sum_matrix():
    j = jnp.arange(2 * LANES)[:, None] % LANES
    n = jnp.arange(2 * LANES)[None, :]
    return jnp.where(n < LANES, (j >= n), True).astype(BF16)


def _sb_kernel(q_ref, k_ref, v_ref, tri_ref, o_ref, *, tq, tk):
    qi = pl.program_id(2)
    q = q_ref[...]
    tri = tri_ref[...]
    lane = lax.broadcasted_iota(jnp.int32, (tq, LANES), 1)
    row = lax.broadcasted_iota(jnp.int32, (tq, LANES), 0)
    n_sub = tk // LANES

    def block(qh, jb, carry, diagonal):
        c, acc = carry
        start = pl.multiple_of(jb * tk, tk)
        kb = k_ref[pl.ds(start, tk), :]
        vb = v_ref[pl.ds(start, tk), :]
        z = _dot_t(qh, kb)
        sp = jnp.maximum(z, 0.0) + jnp.log(1.0 + jnp.exp(-jnp.abs(z)))
        ws = [None] * n_sub
        for s in reversed(range(n_sub)):
            sl = slice(s * LANES, (s + 1) * LANES)
            sp_s = sp[:, sl]
            z_s = z[:, sl]
            if diagonal:
                valid = (lane + s * LANES) < row
                sp_s = jnp.where(valid, sp_s, 0.0)
            hi = sp_s.astype(BF16)
            lo = (sp_s - hi.astype(F32)).astype(BF16)
            r = _dot(jnp.concatenate([hi, lo], axis=1), tri)
            w = jnp.exp(z_s - (c + r[:, :LANES]))
            if diagonal:
                w = jnp.where(valid, w, 0.0)
            ws[s] = w.astype(BF16)
            c = c + r[:, LANES:]
        acc = acc + _dot(jnp.concatenate(ws, axis=1), vb)
        return c, acc

    out = None
    for h in range(2):
        head_lanes = (lane < SB_HEAD_DIM) if h == 0 else (lane >= SB_HEAD_DIM)
        qh = jnp.where(head_lanes, q, jnp.zeros_like(q))
        zero = jnp.zeros((tq, LANES), F32)
        carry = block(qh, qi, (zero, zero), True)
        carry = lax.fori_loop(0, qi, lambda i, cr, qh=qh: block(qh, qi - 1 - i, cr, False), carry)
        out = carry[1] if h == 0 else jnp.where(head_lanes, carry[1], out)
    o_ref[...] = out.astype(o_ref.dtype)


def _sb_attention(sb3, tri):
    bsz, seq, _ = sb3.shape
    tq = min(ATTN_Q, seq)
    tk = tq
    n_pairs = SB_WIDTH // LANES
    return pl.pallas_call(
        functools.partial(_sb_kernel, tq=tq, tk=tk),
        grid=(bsz, n_pairs, seq // tq),
        in_specs=[pl.BlockSpec((None, tq, LANES), lambda b, p, i: (b, i, p)),
                  pl.BlockSpec((None, seq, LANES), lambda b, p, i: (b, 0, n_pairs + p)),
                  pl.BlockSpec((None, seq, LANES), lambda b, p, i: (b, 0, 2 * n_pairs + p)),
                  pl.BlockSpec(tri.shape, lambda b, p, i: (0, 0))],
        out_specs=pl.BlockSpec((None, tq, LANES), lambda b, p, i: (b, i, p)),
        out_shape=jax.ShapeDtypeStruct((bsz, seq, SB_WIDTH), BF16),
        compiler_params=_cparams(("arbitrary", "arbitrary", "arbitrary")),
        name="sb_attn",
    )(sb3, sb3, sb3, tri)


def _mla_kernel(q_ref, k_ref, v_ref, o_ref, *, tq, tk):
    qi = pl.program_id(2)
    lane = lax.broadcasted_iota(jnp.int32, (tq, LANES), 1)
    row = lax.broadcasted_iota(jnp.int32, (tq, tk), 0)
    col = lax.broadcasted_iota(jnp.int32, (tq, tk), 1)

    def block(h, qh, jb, carry, diagonal):
        m, l, acc = carry
        start = pl.multiple_of(jb * tk, tk)
        kb = k_ref[pl.ds(start, tk), h * MLA_HEAD_LANES:(h + 1) * MLA_HEAD_LANES]
        vb = v_ref[pl.ds(start, tk), :]
        s = _dot_t(qh, kb)
        if diagonal:
            s = jnp.where(col <= row, s, -jnp.inf)
        m_new = jnp.maximum(m, jnp.max(s, axis=-1, keepdims=True))
        alpha = jnp.exp(m - m_new)
        p = jnp.exp(s - m_new)
        l = alpha * l + jnp.sum(p, axis=-1, keepdims=True)
        acc = alpha * acc + _dot(p.astype(BF16), vb)
        return m_new, l, acc

    out = None
    for h in range(2):
        qh = q_ref[:, h * MLA_HEAD_LANES:(h + 1) * MLA_HEAD_LANES]
        init = (jnp.full((tq, 1), -jnp.inf, F32), jnp.zeros((tq, 1), F32), jnp.zeros((tq, LANES), F32))
        carry = block(h, qh, qi, init, True)
        carry = lax.fori_loop(0, qi, lambda i, cr, h=h, qh=qh: block(h, qh, qi - 1 - i, cr, False), carry)
        res = carry[2] / carry[1]
        out = res if h == 0 else jnp.where(lane < MLA_V_DIM, out, res)
    o_ref[...] = out.astype(o_ref.dtype)


def _mla_attention(qm3, km3, vm3):
    bsz, seq, _ = qm3.shape
    tq = min(ATTN_Q, seq)
    tk = tq
    n_pairs = MLA_HEADS // 2
    return pl.pallas_call(
        functools.partial(_mla_kernel, tq=tq, tk=tk),
        grid=(bsz, n_pairs, seq // tq),
        in_specs=[pl.BlockSpec((None, tq, 2 * MLA_HEAD_LANES), lambda b, p, i: (b, i, p)),
                  pl.BlockSpec((None, seq, 2 * MLA_HEAD_LANES), lambda b, p, i: (b, 0, p)),
                  pl.BlockSpec((None, seq, LANES), lambda b, p, i: (b, 0, p))],
        out_specs=pl.BlockSpec((None, tq, LANES), lambda b, p, i: (b, i, p)),
        out_shape=jax.ShapeDtypeStruct((bsz, seq, MLA_WIDTH), BF16),
        compiler_params=_cparams(("arbitrary", "arbitrary", "arbitrary")),
        name="mla_attn",
    )(qm3, km3, vm3)


def _layer_norm(v, g, b):
    mu = jnp.mean(v, axis=-1, keepdims=True)
    dv = v - mu
    var = jnp.mean(dv * dv, axis=-1, keepdims=True)
    return dv * lax.rsqrt(var + LN_EPS) * g + b


def _merge_kernel(osb_ref, omla_ref, g_ref, x_ref, gate1_ref, scale2_ref, shift2_ref,
                  wsb_ref, wmla_ref, wout_ref, ln_g_ref, ln_b_ref, wr_ref, br_ref,
                  x1_ref, u2_ref, comb_ref):
    d = x_ref.shape[-1]
    y_sb = _dot(osb_ref[...], wsb_ref[...])
    y_mla = _dot(omla_ref[...], wmla_ref[...])
    mixed = jax.nn.sigmoid(g_ref[:, :d]) * y_sb + jax.nn.sigmoid(g_ref[:, d:]) * y_mla
    attn = _dot(mixed.astype(BF16), wout_ref[...])
    x1 = _layer_norm(DEEPNORM_ALPHA * x_ref[...] + gate1_ref[...] * attn, ln_g_ref[...], ln_b_ref[...])
    x1_ref[...] = x1
    u2 = (x1 * (1.0 + scale2_ref[...]) + shift2_ref[...]).astype(BF16)
    u2_ref[...] = u2

    logits = _dot(u2, wr_ref[...]) + br_ref[...]
    lane = lax.broadcasted_iota(jnp.int32, logits.shape, 1)
    neg = -jnp.inf
    big = jnp.int32(1 << 20)
    is_group = (lane >= N_EXPERTS) & (lane < N_EXPERTS + N_GROUPS)
    g_log = jnp.where(is_group, logits, neg)
    g_max = jnp.max(g_log, axis=-1, keepdims=True)
    g_sum = jnp.sum(jnp.exp(g_log - g_max), axis=-1, keepdims=True)
    p_group = 1.0 / g_sum
    g_idx = jnp.min(jnp.where(g_log == g_max, lane, big), axis=-1, keepdims=True) - N_EXPERTS
    lo = g_idx * EXPERTS_PER_GROUP
    in_group = (lane >= lo) & (lane < lo + EXPERTS_PER_GROUP)
    e_log = jnp.where(in_group, logits, neg)
    v1 = jnp.max(e_log, axis=-1, keepdims=True)
    i1 = jnp.min(jnp.where(e_log == v1, lane, big), axis=-1, keepdims=True)
    e_log2 = jnp.where(lane == i1, neg, e_log)
    v2 = jnp.max(e_log2, axis=-1, keepdims=True)
    i2 = jnp.min(jnp.where(e_log2 == v2, lane, big), axis=-1, keepdims=True)
    e2 = jnp.exp(v2 - v1)
    w1 = p_group / (1.0 + e2)
    w2 = p_group * e2 / (1.0 + e2)
    comb_ref[...] = jnp.where(lane == i1, w1, 0.0) + jnp.where(lane == i2, w2, 0.0)


def _merge(o_sb, o_mla, gates, x2, mod4, w_bsb, w_bmla, w_out, ln_g, ln_b, w_r, b_r, seq):
    n, d = x2.shape
    tm = min(MERGE_ROWS, seq)
    tiles_per_seq = seq // tm

    def whole(a):
        return pl.BlockSpec(a.shape, lambda i: (0,) * a.ndim)

    def mod_spec(k):
        return pl.BlockSpec((None, None, 1, d), lambda i: (i // tiles_per_seq, k, 0, 0))

    def rows(width):
        return pl.BlockSpec((tm, width), lambda i: (i, 0))

    return pl.pallas_call(
        _merge_kernel,
        grid=(n // tm,),
        in_specs=[rows(SB_WIDTH), rows(MLA_WIDTH), rows(2 * d), rows(d), mod_spec(2), mod_spec(4), mod_spec(3),
                  whole(w_bsb), whole(w_bmla), whole(w_out), whole(ln_g), whole(ln_b), whole(w_r), whole(b_r)],
        out_specs=[rows(d), rows(d), rows(LANES)],
        out_shape=[jax.ShapeDtypeStruct((n, d), F32),
                   jax.ShapeDtypeStruct((n, d), BF16),
                   jax.ShapeDtypeStruct((n, LANES), F32)],
        compiler_params=_cparams(("arbitrary",)),
        name="merge",
    )(o_sb, o_mla, gates, x2, mod4, mod4, mod4, w_bsb, w_bmla, w_out, ln_g, ln_b, w_r, b_r)


def _moe_kernel(u2_ref, comb_ref, x1_ref, gate2_ref, wg_ref, wu_ref, wd_ref, ln_g_ref, ln_b_ref,
                o_ref, acc_ref):
    e = pl.program_id(1)

    @pl.when(e == 0)
    def _():
        acc_ref[...] = jnp.zeros_like(acc_ref)

    u2 = u2_ref[...]
    comb = comb_ref[...]
    lane = lax.broadcasted_iota(jnp.int32, comb.shape, 1)
    w_tok = jnp.sum(jnp.where(lane == e, comb, 0.0), axis=-1, keepdims=True)
    hg = _dot(u2, wg_ref[...])
    hu = _dot(u2, wu_ref[...])
    hid = hg * jax.nn.sigmoid(hg) * hu * w_tok
    acc_ref[...] += _dot(hid.astype(BF16), wd_ref[...])

    @pl.when(e == pl.num_programs(1) - 1)
    def _():
        v = DEEPNORM_ALPHA * x1_ref[...] + gate2_ref[...] * acc_ref[...]
        o_ref[...] = _layer_norm(v, ln_g_ref[...], ln_b_ref[...])


def _moe(u2, comb, x1, mod4, w_g, w_u, w_d, ln_g, ln_b, seq):
    n, d = x1.shape
    tm = min(MOE_ROWS, seq)
    tiles_per_seq = seq // tm
    n_exp, _, d_e = w_g.shape

    def rows(width):
        return pl.BlockSpec((tm, width), lambda i, e: (i, 0))

    return pl.pallas_call(
        _moe_kernel,
        grid=(n // tm, n_exp),
        in_specs=[rows(d), rows(LANES), rows(d),
                  pl.BlockSpec((None, None, 1, d), lambda i, e: (i // tiles_per_seq, 5, 0, 0)),
                  pl.BlockSpec((None, d, d_e), lambda i, e: (e, 0, 0)),
                  pl.BlockSpec((None, d, d_e), lambda i, e: (e, 0, 0)),
                  pl.BlockSpec((None, d_e, d), lambda i, e: (e, 0, 0)),
                  pl.BlockSpec((1, d), lambda i, e: (0, 0)),
                  pl.BlockSpec((1, d), lambda i, e: (0, 0))],
        out_specs=rows(d),
        out_shape=jax.ShapeDtypeStruct((n, d), F32),
        scratch_shapes=[pltpu.VMEM((tm, d), F32)],
        compiler_params=_cparams(("arbitrary", "arbitrary")),
        name="moe",
    )(u2, comb, x1, mod4, w_g, w_u, w_d, ln_g, ln_b)


def kernel(x, c, positions, w_ada, b_ada, w_in, mla_q_norm_g, w_q_up, mla_kv_norm_g, w_kv_up, w_branch_sb, w_branch_mla, w_out, ln1_g, ln1_b, w_router_group, b_router_group, w_router_expert, b_router_expert, w_exp_gate, w_exp_up, w_exp_down, ln2_g, ln2_b):
    bsz, seq, d = x.shape
    n = bsz * seq
    invf = _rope_inv_freq_tile()
    posf = positions.astype(F32).reshape(n, 1)
    tri = _suffix_sum_matrix()
    for l in range(w_ada.shape[0]):
        x2 = x.reshape(n, d)
        mod4 = _adaln_mod(c, w_ada[l], b_ada[l]).reshape(bsz, N_MOD, 1, d)
        weights = _prep_in_weights(w_in[l], w_q_up[l], w_kv_up[l])
        sb, qm, km, vm, gates = _in_proj(x2, mod4, posf, invf, weights,
                                         mla_q_norm_g[l].reshape(1, -1), mla_kv_norm_g[l].reshape(1, -1), seq)
        o_sb = _sb_attention(sb.reshape(bsz, seq, -1), tri).reshape(n, SB_WIDTH)
        o_mla = _mla_attention(qm.reshape(bsz, seq, -1), km.reshape(bsz, seq, -1),
                               vm.reshape(bsz, seq, -1)).reshape(n, MLA_WIDTH)
        w_r = jnp.zeros((d, LANES), F32)
        w_r = w_r.at[:, :N_EXPERTS].set(w_router_expert[l]).at[:, N_EXPERTS:N_EXPERTS + N_GROUPS].set(w_router_group[l])
        b_r = jnp.zeros((1, LANES), F32)
        b_r = b_r.at[0, :N_EXPERTS].set(b_router_expert[l]).at[0, N_EXPERTS:N_EXPERTS + N_GROUPS].set(b_router_group[l])
        x1, u2, comb = _merge(o_sb, o_mla, gates, x2, mod4,
                              w_branch_sb[l].astype(BF16), w_branch_mla[l].astype(BF16), w_out[l].astype(BF16),
                              ln1_g[l].reshape(1, d), ln1_b[l].reshape(1, d), w_r.astype(BF16), b_r, seq)
        out = _moe(u2, comb, x1, mod4, w_exp_gate[l].astype(BF16), w_exp_up[l].astype(BF16),
                   w_exp_down[l].astype(BF16), ln2_g[l].reshape(1, d), ln2_b[l].reshape(1, d), seq)
        x = out.reshape(bsz, seq, d)
    return x
```

```python
import functools
import math

import jax
import jax.numpy as jnp
from jax import lax
from jax.experimental import pallas as pl
from jax.experimental.pallas import tpu as pltpu

F32 = jnp.float32
BF16 = jnp.bfloat16

D_MODEL = 1024
SB_HEADS = 8
SB_HEAD_DIM = 64
SB_WIDTH = SB_HEADS * SB_HEAD_DIM
MLA_HEADS = 8
MLA_NOPE_DIM = 64
MLA_ROPE_DIM = 32
MLA_V_DIM = 64
MLA_Q_RANK = 384
MLA_KV_RANK = 256
MLA_QK_DIM = MLA_NOPE_DIM + MLA_ROPE_DIM
MLA_WIDTH = MLA_HEADS * MLA_V_DIM
ROPE_THETA = 10000.0
N_GROUPS = 4
EXPERTS_PER_GROUP = 8
N_EXPERTS = N_GROUPS * EXPERTS_PER_GROUP
D_EXPERT = 256
DEPTH = 1
DEEPNORM_ALPHA = (2.0 * DEPTH) ** 0.25
LN_EPS = 1e-5
RMS_EPS = 1e-6
LOG2_E = math.log2(math.e)
N_MOD = 6

LANES = 128
MLA_HEAD_LANES = 128
VMEM_LIMIT_BYTES = 56 * 1024 * 1024

PROJ_ROWS = 256
ATTN_Q = 256
ATTN_K = 256
MERGE_ROWS = 256
MOE_ROWS = 1024
SB_PAIRS_PER_STEP = 2
MLA_HEADS_PER_STEP = 4


def _cparams(sem):
    return pltpu.CompilerParams(dimension_semantics=sem, vmem_limit_bytes=VMEM_LIMIT_BYTES)


def _dot(a, b):
    return jnp.dot(a, b, preferred_element_type=F32)


def _dot_t(a, b):
    return lax.dot_general(a, b, (((1,), (1,)), ((), ())), preferred_element_type=F32)


def _adaln_kernel(c_ref, w_ref, b_ref, o_ref):
    c = c_ref[...]
    ca = c * jax.nn.sigmoid(c)
    o_ref[...] = _dot(ca.astype(BF16), w_ref[...].astype(BF16)) + b_ref[...]


def _adaln_mod(c, w_ada, b_ada):
    bsz, d = c.shape
    return pl.pallas_call(
        _adaln_kernel,
        grid=(N_MOD,),
        in_specs=[pl.BlockSpec((bsz, d), lambda j: (0, 0)),
                  pl.BlockSpec((d, d), lambda j: (0, j)),
                  pl.BlockSpec((1, d), lambda j: (0, j))],
        out_specs=pl.BlockSpec((bsz, d), lambda j: (0, j)),
        out_shape=jax.ShapeDtypeStruct((bsz, N_MOD * d), F32),
        compiler_params=_cparams(("arbitrary",)),
        name="adaln_mod",
    )(c, w_ada, b_ada.reshape(1, N_MOD * d))


def _in_proj_kernel(x_ref, scale_ref, shift_ref, pos_ref, invf_ref,
                    w_sb_ref, w_dn_ref, w_kr_ref, w_g_ref, qg_ref, kvg_ref,
                    w_qu_ref, w_ku_ref, w_vu_ref,
                    sb_ref, qm_ref, km_ref, vm_ref, g_ref):
    u = (x_ref[...] * (1.0 + scale_ref[...]) + shift_ref[...]).astype(BF16)

    sb = _dot(u, w_sb_ref[...])
    sb_scale = LOG2_E / math.sqrt(SB_HEAD_DIM)
    sb_ref[:, :SB_WIDTH] = (sb[:, :SB_WIDTH] * sb_scale).astype(BF16)
    sb_ref[:, SB_WIDTH:] = sb[:, SB_WIDTH:].astype(BF16)

    g_ref[...] = _dot(u, w_g_ref[...])

    ang = pos_ref[...] * invf_ref[...]
    cos = jnp.cos(ang)
    sin = jnp.sin(ang)

    dn = _dot(u, w_dn_ref[...])
    q_dn = dn[:, :MLA_Q_RANK]
    kv_dn = dn[:, MLA_Q_RANK:]
    qn = q_dn * lax.rsqrt(jnp.mean(q_dn * q_dn, axis=-1, keepdims=True) + RMS_EPS) * qg_ref[...]
    kvn = kv_dn * lax.rsqrt(jnp.mean(kv_dn * kv_dn, axis=-1, keepdims=True) + RMS_EPS) * kvg_ref[...]
    qn = qn.astype(BF16)
    kvn = kvn.astype(BF16)

    width = MLA_HEADS * MLA_HEAD_LANES
    qu = _dot(qn, w_qu_ref[...])
    ku = _dot(kvn, w_ku_ref[...])
    vu = _dot(kvn, w_vu_ref[...])
    v_lane = lax.broadcasted_iota(jnp.int32, vu.shape, 1) % MLA_HEAD_LANES
    vm_ref[...] = jnp.where(v_lane < MLA_V_DIM, vu, 1.0).astype(BF16)
    kr = _dot(u, w_kr_ref[...])
    kr = kr[:, :LANES] * cos + kr[:, LANES:] * sin
    mla_scale = LOG2_E / math.sqrt(MLA_QK_DIM)
    for h in range(MLA_HEADS):
        sl = slice(h * MLA_HEAD_LANES, (h + 1) * MLA_HEAD_LANES)
        sl_rot = slice(width + h * MLA_HEAD_LANES, width + (h + 1) * MLA_HEAD_LANES)
        qm_ref[:, sl] = ((qu[:, sl] * cos + qu[:, sl_rot] * sin) * mla_scale).astype(BF16)
        km_ref[:, sl] = (ku[:, sl] + kr).astype(BF16)


def _rope_inv_freq_tile():
    inv_freq = 1.0 / (ROPE_THETA ** (jnp.arange(0, MLA_ROPE_DIM, 2, dtype=F32) / MLA_ROPE_DIM))
    tile = jnp.zeros((LANES,), F32)
    half = MLA_ROPE_DIM // 2
    tile = tile.at[MLA_NOPE_DIM:MLA_NOPE_DIM + half].set(inv_freq)
    tile = tile.at[MLA_NOPE_DIM + half:MLA_NOPE_DIM + 2 * half].set(inv_freq)
    return tile.reshape(1, LANES)


def _rotate_half_cols(w):
    half = MLA_ROPE_DIM // 2
    return jnp.concatenate([-w[..., half:], w[..., :half]], axis=-1)


def _prep_in_weights(w_in, w_q_up, w_kv_up):
    d = w_in.shape[0]
    o = 0
    w_sb = w_in[:, o:o + 3 * SB_WIDTH]; o += 3 * SB_WIDTH
    w_dn = w_in[:, o:o + MLA_Q_RANK + MLA_KV_RANK]; o += MLA_Q_RANK + MLA_KV_RANK
    w_kr = w_in[:, o:o + MLA_ROPE_DIM]; o += MLA_ROPE_DIM
    w_g = w_in[:, o:]

    def rope_tile(w):
        z = jnp.zeros((w.shape[0], LANES), w.dtype)
        return z.at[:, MLA_NOPE_DIM:MLA_NOPE_DIM + MLA_ROPE_DIM].set(w)

    w_kr2 = jnp.concatenate([rope_tile(w_kr), rope_tile(_rotate_half_cols(w_kr))], axis=1)

    wq = w_q_up.reshape(MLA_Q_RANK, MLA_HEADS, MLA_QK_DIM)
    q_base = jnp.zeros((MLA_Q_RANK, MLA_HEADS, MLA_HEAD_LANES), w_q_up.dtype).at[:, :, :MLA_QK_DIM].set(wq)
    q_rot = jnp.zeros((MLA_Q_RANK, MLA_HEADS, MLA_HEAD_LANES), w_q_up.dtype)
    q_rot = q_rot.at[:, :, MLA_NOPE_DIM:MLA_QK_DIM].set(_rotate_half_cols(wq[:, :, MLA_NOPE_DIM:]))
    w_qu = jnp.concatenate([q_base.reshape(MLA_Q_RANK, -1), q_rot.reshape(MLA_Q_RANK, -1)], axis=1)

    wkv = w_kv_up.reshape(MLA_KV_RANK, MLA_HEADS, MLA_NOPE_DIM + MLA_V_DIM)
    w_ku = jnp.zeros((MLA_KV_RANK, MLA_HEADS, MLA_HEAD_LANES), w_kv_up.dtype)
    w_ku = w_ku.at[:, :, :MLA_NOPE_DIM].set(wkv[:, :, :MLA_NOPE_DIM]).reshape(MLA_KV_RANK, -1)
    w_vu = jnp.zeros((MLA_KV_RANK, MLA_HEADS, MLA_HEAD_LANES), w_kv_up.dtype)
    w_vu = w_vu.at[:, :, :MLA_V_DIM].set(wkv[:, :, MLA_NOPE_DIM:]).reshape(MLA_KV_RANK, -1)
    del d
    return tuple(w.astype(BF16) for w in (w_sb, w_dn, w_kr2, w_g, w_qu, w_ku, w_vu))


def _in_proj(x2, mod4, posf, invf, weights, qg, kvg, seq):
    n, d = x2.shape
    tm = min(PROJ_ROWS, seq)
    tiles_per_seq = seq // tm
    w_sb, w_dn, w_kr2, w_g, w_qu, w_ku, w_vu = weights

    def whole(a):
        return pl.BlockSpec(a.shape, lambda i: (0,) * a.ndim)

    def mod_spec(k):
        return pl.BlockSpec((None, None, 1, d), lambda i: (i // tiles_per_seq, k, 0, 0))

    def rows(width):
        return pl.BlockSpec((tm, width), lambda i: (i, 0))

    width = MLA_HEADS * MLA_HEAD_LANES
    return pl.pallas_call(
        _in_proj_kernel,
        grid=(n // tm,),
        in_specs=[rows(d), mod_spec(1), mod_spec(0), rows(1), whole(invf),
                  whole(w_sb), whole(w_dn), whole(w_kr2), whole(w_g), whole(qg), whole(kvg),
                  whole(w_qu), whole(w_ku), whole(w_vu)],
        out_specs=[rows(3 * SB_WIDTH), rows(width), rows(width), rows(width), rows(2 * d)],
        out_shape=[jax.ShapeDtypeStruct((n, 3 * SB_WIDTH), BF16),
                   jax.ShapeDtypeStruct((n, width), BF16),
                   jax.ShapeDtypeStruct((n, width), BF16),
                   jax.ShapeDtypeStruct((n, width), BF16),
                   jax.ShapeDtypeStruct((n, 2 * d), F32)],
        compiler_params=_cparams(("arbitrary",)),
        name="in_proj",
    )(x2, mod4, mod4, posf, invf, w_sb, w_dn, w_kr2, w_g, qg, kvg, w_qu, w_ku, w_vu)


def _suffix_sum_matrix():
    j = jnp.arange(2 * LANES)[:, None] % LANES
    n = jnp.arange(2 * LANES)[None, :]
    return jnp.where(n < LANES, (j >= n), True).astype(F32)


def _sb_kernel(q_ref, k_ref, v_ref, tri_ref, o_ref, qh_ref, c_ref, acc_ref, z_ref, w_ref,
               *, tq, tk, n_pairs):
    qi = pl.program_id(2)
    lane = lax.broadcasted_iota(jnp.int32, (tq, LANES), 1)
    row = lax.broadcasted_iota(jnp.int32, (tq, LANES), 0)
    n_sub = tk // LANES
    n_chains = 2 * n_pairs

    for p in range(n_pairs):
        qp = q_ref[:, p * LANES:(p + 1) * LANES]
        for h in range(2):
            head_lanes = (lane < SB_HEAD_DIM) if h == 0 else (lane >= SB_HEAD_DIM)
            qh_ref[2 * p + h] = jnp.where(head_lanes, qp, jnp.zeros_like(qp))
    c_ref[...] = jnp.zeros_like(c_ref)
    acc_ref[...] = jnp.zeros_like(acc_ref)

    def key_rows(t):
        return pl.ds(pl.multiple_of((qi - t) * tk, tk), tk)

    def pair_lanes(ci):
        return slice((ci // 2) * LANES, (ci // 2 + 1) * LANES)

    def valid_keys(s):
        return (lane + s * LANES) < row

    def scores(t, slot, diagonal):
        del diagonal
        for ci in range(n_chains):
            z_ref[slot, ci] = _dot_t(qh_ref[ci], k_ref[key_rows(t), pair_lanes(ci)])

    def weights(t, slot, diagonal):
        del t
        tri = tri_ref[...]
        sign = jnp.uint32(0x80000000)
        bf16_bits = jnp.uint32(0xFFFF0000)
        for ci in range(n_chains):
            c = c_ref[ci]
            for s in reversed(range(n_sub)):
                z = z_ref[slot, ci, :, s * LANES:(s + 1) * LANES]
                neg_abs = pltpu.bitcast(pltpu.bitcast(z, jnp.uint32) | sign, F32)
                sp = jnp.maximum(z, 0.0) + jnp.log(1.0 + jnp.exp2(neg_abs)) * LOG2_E
                if diagonal:
                    sp = jnp.where(valid_keys(s), sp, 0.0)
                hi = pltpu.bitcast(pltpu.bitcast(sp, jnp.uint32) & bf16_bits, F32)
                r = _dot(jnp.concatenate([hi, sp - hi], axis=1), tri)
                w = jnp.exp2(z - (c + r[:, :LANES]))
                if diagonal:
                    w = jnp.where(valid_keys(s), w, 0.0)
                w_ref[slot, ci, :, s * LANES:(s + 1) * LANES] = w.astype(BF16)
                c = c + r[:, LANES:]
            c_ref[ci] = c

    def values(t, slot):
        for ci in range(n_chains):
            acc_ref[ci] += _dot(w_ref[slot, ci], v_ref[key_rows(t), pair_lanes(ci)])

    _pipeline3(qi + 1, scores, weights, values)

    for p in range(n_pairs):
        out = jnp.where(lane < SB_HEAD_DIM, acc_ref[2 * p], acc_ref[2 * p + 1])
        o_ref[:, p * LANES:(p + 1) * LANES] = out.astype(o_ref.dtype)


def _sb_attention(sb3, tri):
    bsz, seq, _ = sb3.shape
    tq = min(ATTN_Q, seq)
    tk = tq
    n_pairs = SB_PAIRS_PER_STEP
    width = n_pairs * LANES
    groups = SB_WIDTH // width
    return pl.pallas_call(
        functools.partial(_sb_kernel, tq=tq, tk=tk, n_pairs=n_pairs),
        grid=(bsz, groups, seq // tq),
        in_specs=[pl.BlockSpec((None, tq, width), lambda b, g, i: (b, i, g)),
                  pl.BlockSpec((None, seq, width), lambda b, g, i: (b, 0, groups + g)),
                  pl.BlockSpec((None, seq, width), lambda b, g, i: (b, 0, 2 * groups + g)),
                  pl.BlockSpec(tri.shape, lambda b, g, i: (0, 0))],
        out_specs=pl.BlockSpec((None, tq, width), lambda b, g, i: (b, i, g)),
        out_shape=jax.ShapeDtypeStruct((bsz, seq, SB_WIDTH), BF16),
        scratch_shapes=[pltpu.VMEM((2 * n_pairs, tq, LANES), BF16),
                        pltpu.VMEM((2 * n_pairs, tq, LANES), F32),
                        pltpu.VMEM((2 * n_pairs, tq, LANES), F32),
                        pltpu.VMEM((2, 2 * n_pairs, tq, tk), F32),
                        pltpu.VMEM((2, 2 * n_pairs, tq, tk), BF16)],
        compiler_params=_cparams(("arbitrary", "arbitrary", "arbitrary")),
        name="sb_attn",
    )(sb3, sb3, sb3, tri)


def _pipeline3(n, stage_a, stage_b, stage_c):
    stage_a(0, 0, True)

    @pl.when(n >= 2)
    def _():
        stage_a(1, 1, False)

    stage_b(0, 0, True)

    def step(i, slot):
        stage_c(i - 2, slot)
        stage_b(i - 1, 1 - slot, False)
        stage_a(i, slot, False)

    def body(k, carry):
        step(2 + 2 * k, 0)
        step(3 + 2 * k, 1)
        return carry

    lax.fori_loop(0, jnp.maximum(n - 2, 0) // 2, body, 0)

    @pl.when((n >= 3) & (n % 2 == 1))
    def _():
        step(n - 1, 0)

    @pl.when(n >= 2)
    def _():
        stage_c(n - 2, n % 2)
        stage_b(n - 1, (n - 1) % 2, False)

    stage_c(n - 1, (n - 1) % 2)


def _mla_kernel(q_ref, k_ref, v_ref, o_ref, m_ref, acc_ref, s_ref, p_ref, alpha_ref, *, tq, tk, n_heads):
    qi = pl.program_id(2)
    lane = lax.broadcasted_iota(jnp.int32, (tq, LANES), 1)
    n_sub = tk // LANES
    m_ref[...] = jnp.full_like(m_ref, -jnp.inf)
    acc_ref[...] = jnp.zeros_like(acc_ref)

    def key_rows(t):
        return pl.ds(pl.multiple_of((qi - t) * tk, tk), tk)

    def head_lanes(ci):
        return slice(ci * MLA_HEAD_LANES, (ci + 1) * MLA_HEAD_LANES)

    def scores(t, slot, diagonal):
        for ci in range(n_heads):
            s = _dot_t(q_ref[:, head_lanes(ci)], k_ref[key_rows(t), head_lanes(ci)])
            if diagonal:
                row = lax.broadcasted_iota(jnp.int32, (tq, tk), 0)
                col = lax.broadcasted_iota(jnp.int32, (tq, tk), 1)
                s = jnp.where(col <= row, s, -jnp.inf)
            s_ref[slot, ci] = s

    def numerators(t, slot, diagonal):
        del t, diagonal
        for ci in range(n_heads):
            s = s_ref[slot, ci]
            m = m_ref[ci]
            m_new = jnp.maximum(m, jnp.broadcast_to(jnp.max(s, axis=-1, keepdims=True), (tq, LANES)))
            alpha_ref[slot, ci] = jnp.exp2(m - m_new)
            for j in range(n_sub):
                sl = slice(j * LANES, (j + 1) * LANES)
                p_ref[slot, ci, :, sl] = jnp.exp2(s[:, sl] - m_new).astype(BF16)
            m_ref[ci] = m_new

    def values(t, slot):
        for ci in range(n_heads):
            pv = _dot(p_ref[slot, ci], v_ref[key_rows(t), head_lanes(ci)])
            acc_ref[ci] = alpha_ref[slot, ci] * acc_ref[ci] + pv

    _pipeline3(qi + 1, scores, numerators, values)

    for p in range(n_heads // 2):
        res = []
        for h in range(2):
            acc = acc_ref[2 * p + h]
            res.append(acc / pltpu.roll(acc, MLA_V_DIM, 1))
        out = jnp.where(lane < MLA_V_DIM, res[0], pltpu.roll(res[1], MLA_V_DIM, 1))
        o_ref[:, p * LANES:(p + 1) * LANES] = out.astype(o_ref.dtype)


def _mla_attention(qm3, km3, vm3):
    bsz, seq, _ = qm3.shape
    tq = min(ATTN_Q, seq)
    tk = tq
    n_heads = MLA_HEADS_PER_STEP
    groups = MLA_HEADS // n_heads
    qk_width = n_heads * MLA_HEAD_LANES
    o_width = n_heads * MLA_V_DIM
    return pl.pallas_call(
        functools.partial(_mla_kernel, tq=tq, tk=tk, n_heads=n_heads),
        grid=(bsz, groups, seq // tq),
        in_specs=[pl.BlockSpec((None, tq, qk_width), lambda b, g, i: (b, i, g)),
                  pl.BlockSpec((None, seq, qk_width), lambda b, g, i: (b, 0, g)),
                  pl.BlockSpec((None, seq, qk_width), lambda b, g, i: (b, 0, g))],
        out_specs=pl.BlockSpec((None, tq, o_width), lambda b, g, i: (b, i, g)),
        out_shape=jax.ShapeDtypeStruct((bsz, seq, MLA_WIDTH), BF16),
        scratch_shapes=[pltpu.VMEM((n_heads, tq, LANES), F32),
                        pltpu.VMEM((n_heads, tq, LANES), F32),
                        pltpu.VMEM((2, n_heads, tq, tk), F32),
                        pltpu.VMEM((2, n_heads, tq, tk), BF16),
                        pltpu.VMEM((2, n_heads, tq, LANES), F32)],
        compiler_params=_cparams(("arbitrary", "arbitrary", "arbitrary")),
        name="mla_attn",
    )(qm3, km3, vm3)


def _layer_norm(v, g, b):
    mu = jnp.mean(v, axis=-1, keepdims=True)
    dv = v - mu
    var = jnp.mean(dv * dv, axis=-1, keepdims=True)
    return dv * lax.rsqrt(var + LN_EPS) * g + b


def _merge_kernel(osb_ref, omla_ref, g_ref, x_ref, gate1_ref, scale2_ref, shift2_ref,
                  wsb_ref, wmla_ref, wout_ref, ln_g_ref, ln_b_ref, wr_ref, br_ref,
                  x1_ref, u2_ref, comb_ref):
    d = x_ref.shape[-1]
    y_sb = _dot(osb_ref[...], wsb_ref[...])
    y_mla = _dot(omla_ref[...], wmla_ref[...])
    mixed = jax.nn.sigmoid(g_ref[:, :d]) * y_sb + jax.nn.sigmoid(g_ref[:, d:]) * y_mla
    attn = _dot(mixed.astype(BF16), wout_ref[...])
    x1 = _layer_norm(DEEPNORM_ALPHA * x_ref[...] + gate1_ref[...] * attn, ln_g_ref[...], ln_b_ref[...])
    x1_ref[...] = x1
    u2 = (x1 * (1.0 + scale2_ref[...]) + shift2_ref[...]).astype(BF16)
    u2_ref[...] = u2

    logits = _dot(u2, wr_ref[...]) + br_ref[...]
    lane = lax.broadcasted_iota(jnp.int32, logits.shape, 1)
    neg = -jnp.inf
    big = jnp.int32(1 << 20)
    is_group = (lane >= N_EXPERTS) & (lane < N_EXPERTS + N_GROUPS)
    g_log = jnp.where(is_group, logits, neg)
    g_max = jnp.max(g_log, axis=-1, keepdims=True)
    g_sum = jnp.sum(jnp.exp(g_log - g_max), axis=-1, keepdims=True)
    p_group = 1.0 / g_sum
    g_idx = jnp.min(jnp.where(g_log == g_max, lane, big), axis=-1, keepdims=True) - N_EXPERTS
    lo = g_idx * EXPERTS_PER_GROUP
    in_group = (lane >= lo) & (lane < lo + EXPERTS_PER_GROUP)
    e_log = jnp.where(in_group, logits, neg)
    v1 = jnp.max(e_log, axis=-1, keepdims=True)
    i1 = jnp.min(jnp.where(e_log == v1, lane, big), axis=-1, keepdims=True)
    e_log2 = jnp.where(lane == i1, neg, e_log)
    v2 = jnp.max(e_log2, axis=-1, keepdims=True)
    i2 = jnp.min(jnp.where(e_log2 == v2, lane, big), axis=-1, keepdims=True)
    e2 = jnp.exp(v2 - v1)
    w1 = p_group / (1.0 + e2)
    w2 = p_group * e2 / (1.0 + e2)
    comb_ref[...] = jnp.where(lane == i1, w1, 0.0) + jnp.where(lane == i2, w2, 0.0)


def _merge(o_sb, o_mla, gates, x2, mod4, w_bsb, w_bmla, w_out, ln_g, ln_b, w_r, b_r, seq):
    n, d = x2.shape
    tm = min(MERGE_ROWS, seq)
    tiles_per_seq = seq // tm

    def whole(a):
        return pl.BlockSpec(a.shape, lambda i: (0,) * a.ndim)

    def mod_spec(k):
        return pl.BlockSpec((None, None, 1, d), lambda i: (i // tiles_per_seq, k, 0, 0))

    def rows(width):
        return pl.BlockSpec((tm, width), lambda i: (i, 0))

    return pl.pallas_call(
        _merge_kernel,
        grid=(n // tm,),
        in_specs=[rows(SB_WIDTH), rows(MLA_WIDTH), rows(2 * d), rows(d), mod_spec(2), mod_spec(4), mod_spec(3),
                  whole(w_bsb), whole(w_bmla), whole(w_out), whole(ln_g), whole(ln_b), whole(w_r), whole(b_r)],
        out_specs=[rows(d), rows(d), rows(LANES)],
        out_shape=[jax.ShapeDtypeStruct((n, d), F32),
                   jax.ShapeDtypeStruct((n, d), BF16),
                   jax.ShapeDtypeStruct((n, LANES), F32)],
        compiler_params=_cparams(("arbitrary",)),
        name="merge",
    )(o_sb, o_mla, gates, x2, mod4, mod4, mod4, w_bsb, w_bmla, w_out, ln_g, ln_b, w_r, b_r)


def _moe_kernel(u2_ref, comb_ref, x1_ref, gate2_ref, wg_ref, wu_ref, wd_ref, ln_g_ref, ln_b_ref,
                o_ref, acc_ref):
    e = pl.program_id(1)

    @pl.when(e == 0)
    def _():
        acc_ref[...] = jnp.zeros_like(acc_ref)

    u2 = u2_ref[...]
    comb = comb_ref[...]
    lane = lax.broadcasted_iota(jnp.int32, comb.shape, 1)
    w_tok = jnp.sum(jnp.where(lane == e, comb, 0.0), axis=-1, keepdims=True)
    hg = _dot(u2, wg_ref[...])
    hu = _dot(u2, wu_ref[...])
    hid = hg * jax.nn.sigmoid(hg) * hu * w_tok
    acc_ref[...] += _dot(hid.astype(BF16), wd_ref[...])

    @pl.when(e == pl.num_programs(1) - 1)
    def _():
        v = DEEPNORM_ALPHA * x1_ref[...] + gate2_ref[...] * acc_ref[...]
        o_ref[...] = _layer_norm(v, ln_g_ref[...], ln_b_ref[...])


def _moe(u2, comb, x1, mod4, w_g, w_u, w_d, ln_g, ln_b, seq):
    n, d = x1.shape
    tm = min(MOE_ROWS, seq)
    tiles_per_seq = seq // tm
    n_exp, _, d_e = w_g.shape

    def rows(width):
        return pl.BlockSpec((tm, width), lambda i, e: (i, 0))

    return pl.pallas_call(
        _moe_kernel,
        grid=(n // tm, n_exp),
        in_specs=[rows(d), rows(LANES), rows(d),
                  pl.BlockSpec((None, None, 1, d), lambda i, e: (i // tiles_per_seq, 5, 0, 0)),
                  pl.BlockSpec((None, d, d_e), lambda i, e: (e, 0, 0)),
                  pl.BlockSpec((None, d, d_e), lambda i, e: (e, 0, 0)),
                  pl.BlockSpec((None, d_e, d), lambda i, e: (e, 0, 0)),
                  pl.BlockSpec((1, d), lambda i, e: (0, 0)),
                  pl.BlockSpec((1, d), lambda i, e: (0, 0))],
        out_specs=rows(d),
        out_shape=jax.ShapeDtypeStruct((n, d), F32),
        scratch_shapes=[pltpu.VMEM((tm, d), F32)],
        compiler_params=_cparams(("arbitrary", "arbitrary")),
        name="moe",
    )(u2, comb, x1, mod4, w_g, w_u, w_d, ln_g, ln_b)


def kernel(x, c, positions, w_ada, b_ada, w_in, mla_q_norm_g, w_q_up, mla_kv_norm_g, w_kv_up, w_branch_sb, w_branch_mla, w_out, ln1_g, ln1_b, w_router_group, b_router_group, w_router_expert, b_router_expert, w_exp_gate, w_exp_up, w_exp_down, ln2_g, ln2_b):
    bsz, seq, d = x.shape
    n = bsz * seq
    invf = _rope_inv_freq_tile()
    posf = positions.astype(F32).reshape(n, 1)
    tri = _suffix_sum_matrix()
    for l in range(w_ada.shape[0]):
        x2 = x.reshape(n, d)
        mod4 = _adaln_mod(c, w_ada[l], b_ada[l]).reshape(bsz, N_MOD, 1, d)
        weights = _prep_in_weights(w_in[l], w_q_up[l], w_kv_up[l])
        sb, qm, km, vm, gates = _in_proj(x2, mod4, posf, invf, weights,
                                         mla_q_norm_g[l].reshape(1, -1), mla_kv_norm_g[l].reshape(1, -1), seq)
        o_sb = _sb_attention(sb.reshape(bsz, seq, -1), tri).reshape(n, SB_WIDTH)
        o_mla = _mla_attention(qm.reshape(bsz, seq, -1), km.reshape(bsz, seq, -1),
                               vm.reshape(bsz, seq, -1)).reshape(n, MLA_WIDTH)
        w_r = jnp.zeros((d, LANES), F32)
        w_r = w_r.at[:, :N_EXPERTS].set(w_router_expert[l]).at[:, N_EXPERTS:N_EXPERTS + N_GROUPS].set(w_router_group[l])
        b_r = jnp.zeros((1, LANES), F32)
        b_r = b_r.at[0, :N_EXPERTS].set(b_router_expert[l]).at[0, N_EXPERTS:N_EXPERTS + N_GROUPS].set(b_router_group[l])
        x1, u2, comb = _merge(o_sb, o_mla, gates, x2, mod4,
                              w_branch_sb[l].astype(BF16), w_branch_mla[l].astype(BF16), w_out[l].astype(BF16),
                              ln1_g[l].reshape(1, d), ln1_b[l].reshape(1, d), w_r.astype(BF16), b_r, seq)
        out = _moe(u2, comb, x1, mod4, w_exp_gate[l].astype(BF16), w_exp_up[l].astype(BF16),
                   w_exp_down[l].astype(BF16), ln2_g[l].reshape(1, d), ln2_b[l].reshape(1, d), seq)
        x = out.reshape(bsz, seq, d)
    return x
```

```python
import functools
import math

import jax
import jax.numpy as jnp
from jax import lax
from jax.experimental import pallas as pl
from jax.experimental.pallas import tpu as pltpu

F32 = jnp.float32
BF16 = jnp.bfloat16

D_MODEL = 1024
SB_HEADS = 8
SB_HEAD_DIM = 64
SB_WIDTH = SB_HEADS * SB_HEAD_DIM
MLA_HEADS = 8
MLA_NOPE_DIM = 64
MLA_ROPE_DIM = 32
MLA_V_DIM = 64
MLA_Q_RANK = 384
MLA_KV_RANK = 256
MLA_QK_DIM = MLA_NOPE_DIM + MLA_ROPE_DIM
MLA_WIDTH = MLA_HEADS * MLA_V_DIM
ROPE_THETA = 10000.0
N_GROUPS = 4
EXPERTS_PER_GROUP = 8
N_EXPERTS = N_GROUPS * EXPERTS_PER_GROUP
D_EXPERT = 256
DEPTH = 1
DEEPNORM_ALPHA = (2.0 * DEPTH) ** 0.25
LN_EPS = 1e-5
RMS_EPS = 1e-6
LOG2_E = math.log2(math.e)
N_MOD = 6

LANES = 128
MLA_HEAD_LANES = 128
VMEM_LIMIT_BYTES = 56 * 1024 * 1024

PROJ_ROWS = 256
ATTN_Q = 256
ATTN_K = 256
MERGE_ROWS = 256
MOE_ROWS = 512
LN2_ROWS = 512
SB_PAIRS_PER_STEP = 2
MLA_HEADS_PER_STEP = 4


def _cparams(sem):
    return pltpu.CompilerParams(dimension_semantics=sem, vmem_limit_bytes=VMEM_LIMIT_BYTES)


def _dot(a, b):
    return jnp.dot(a, b, preferred_element_type=F32)


def _dot_t(a, b):
    return lax.dot_general(a, b, (((1,), (1,)), ((), ())), preferred_element_type=F32)


def _adaln_kernel(c_ref, w_ref, b_ref, o_ref):
    c = c_ref[...]
    ca = c * jax.nn.sigmoid(c)
    o_ref[...] = _dot(ca.astype(BF16), w_ref[...].astype(BF16)) + b_ref[...]


def _adaln_mod(c, w_ada, b_ada):
    bsz, d = c.shape
    return pl.pallas_call(
        _adaln_kernel,
        grid=(N_MOD,),
        in_specs=[pl.BlockSpec((bsz, d), lambda j: (0, 0)),
                  pl.BlockSpec((d, d), lambda j: (0, j)),
                  pl.BlockSpec((1, d), lambda j: (0, j))],
        out_specs=pl.BlockSpec((bsz, d), lambda j: (0, j)),
        out_shape=jax.ShapeDtypeStruct((bsz, N_MOD * d), F32),
        compiler_params=_cparams(("arbitrary",)),
        name="adaln_mod",
    )(c, w_ada, b_ada.reshape(1, N_MOD * d))


def _in_proj_kernel(x_ref, scale_ref, shift_ref, pos_ref, invf_ref,
                    w_sb_ref, w_dn_ref, w_kr_ref, w_g_ref, qg_ref, kvg_ref,
                    w_qu_ref, w_ku_ref, w_vu_ref,
                    sb_ref, qm_ref, km_ref, vm_ref, g_ref):
    u = (x_ref[...] * (1.0 + scale_ref[...]) + shift_ref[...]).astype(BF16)

    sb = _dot(u, w_sb_ref[...])
    sb_scale = LOG2_E / math.sqrt(SB_HEAD_DIM)
    sb_ref[:, :SB_WIDTH] = (sb[:, :SB_WIDTH] * sb_scale).astype(BF16)
    sb_ref[:, SB_WIDTH:] = sb[:, SB_WIDTH:].astype(BF16)

    g_ref[...] = _dot(u, w_g_ref[...])

    ang = pos_ref[...] * invf_ref[...]
    cos = jnp.cos(ang)
    sin = jnp.sin(ang)

    dn = _dot(u, w_dn_ref[...])
    q_dn = dn[:, :MLA_Q_RANK]
    kv_dn = dn[:, MLA_Q_RANK:]
    qn = q_dn * lax.rsqrt(jnp.mean(q_dn * q_dn, axis=-1, keepdims=True) + RMS_EPS) * qg_ref[...]
    kvn = kv_dn * lax.rsqrt(jnp.mean(kv_dn * kv_dn, axis=-1, keepdims=True) + RMS_EPS) * kvg_ref[...]
    qn = qn.astype(BF16)
    kvn = kvn.astype(BF16)

    width = MLA_HEADS * MLA_HEAD_LANES
    qu = _dot(qn, w_qu_ref[...])
    ku = _dot(kvn, w_ku_ref[...])
    vu = _dot(kvn, w_vu_ref[...])
    v_lane = lax.broadcasted_iota(jnp.int32, vu.shape, 1) % MLA_HEAD_LANES
    vm_ref[...] = jnp.where(v_lane < MLA_V_DIM, vu, 1.0).astype(BF16)
    kr = _dot(u, w_kr_ref[...])
    kr = kr[:, :LANES] * cos + kr[:, LANES:] * sin
    mla_scale = LOG2_E / math.sqrt(MLA_QK_DIM)
    for h in range(MLA_HEADS):
        sl = slice(h * MLA_HEAD_LANES, (h + 1) * MLA_HEAD_LANES)
        sl_rot = slice(width + h * MLA_HEAD_LANES, width + (h + 1) * MLA_HEAD_LANES)
        qm_ref[:, sl] = ((qu[:, sl] * cos + qu[:, sl_rot] * sin) * mla_scale).astype(BF16)
        km_ref[:, sl] = (ku[:, sl] + kr).astype(BF16)


def _rope_inv_freq_tile():
    inv_freq = 1.0 / (ROPE_THETA ** (jnp.arange(0, MLA_ROPE_DIM, 2, dtype=F32) / MLA_ROPE_DIM))
    tile = jnp.zeros((LANES,), F32)
    half = MLA_ROPE_DIM // 2
    tile = tile.at[MLA_NOPE_DIM:MLA_NOPE_DIM + half].set(inv_freq)
    tile = tile.at[MLA_NOPE_DIM + half:MLA_NOPE_DIM + 2 * half].set(inv_freq)
    return tile.reshape(1, LANES)


def _rotate_half_cols(w):
    half = MLA_ROPE_DIM // 2
    return jnp.concatenate([-w[..., half:], w[..., :half]], axis=-1)


def _prep_in_weights(w_in, w_q_up, w_kv_up):
    d = w_in.shape[0]
    o = 0
    w_sb = w_in[:, o:o + 3 * SB_WIDTH]; o += 3 * SB_WIDTH
    w_dn = w_in[:, o:o + MLA_Q_RANK + MLA_KV_RANK]; o += MLA_Q_RANK + MLA_KV_RANK
    w_kr = w_in[:, o:o + MLA_ROPE_DIM]; o += MLA_ROPE_DIM
    w_g = w_in[:, o:]

    def rope_tile(w):
        z = jnp.zeros((w.shape[0], LANES), w.dtype)
        return z.at[:, MLA_NOPE_DIM:MLA_NOPE_DIM + MLA_ROPE_DIM].set(w)

    w_kr2 = jnp.concatenate([rope_tile(w_kr), rope_tile(_rotate_half_cols(w_kr))], axis=1)

    wq = w_q_up.reshape(MLA_Q_RANK, MLA_HEADS, MLA_QK_DIM)
    q_base = jnp.zeros((MLA_Q_RANK, MLA_HEADS, MLA_HEAD_LANES), w_q_up.dtype).at[:, :, :MLA_QK_DIM].set(wq)
    q_rot = jnp.zeros((MLA_Q_RANK, MLA_HEADS, MLA_HEAD_LANES), w_q_up.dtype)
    q_rot = q_rot.at[:, :, MLA_NOPE_DIM:MLA_QK_DIM].set(_rotate_half_cols(wq[:, :, MLA_NOPE_DIM:]))
    w_qu = jnp.concatenate([q_base.reshape(MLA_Q_RANK, -1), q_rot.reshape(MLA_Q_RANK, -1)], axis=1)

    wkv = w_kv_up.reshape(MLA_KV_RANK, MLA_HEADS, MLA_NOPE_DIM + MLA_V_DIM)
    w_ku = jnp.zeros((MLA_KV_RANK, MLA_HEADS, MLA_HEAD_LANES), w_kv_up.dtype)
    w_ku = w_ku.at[:, :, :MLA_NOPE_DIM].set(wkv[:, :, :MLA_NOPE_DIM]).reshape(MLA_KV_RANK, -1)
    w_vu = jnp.zeros((MLA_KV_RANK, MLA_HEADS, MLA_HEAD_LANES), w_kv_up.dtype)
    w_vu = w_vu.at[:, :, :MLA_V_DIM].set(wkv[:, :, MLA_NOPE_DIM:]).reshape(MLA_KV_RANK, -1)
    del d
    return tuple(w.astype(BF16) for w in (w_sb, w_dn, w_kr2, w_g, w_qu, w_ku, w_vu))


def _in_proj(x2, mod4, posf, invf, weights, qg, kvg, seq):
    n, d = x2.shape
    tm = min(PROJ_ROWS, seq)
    tiles_per_seq = seq // tm
    w_sb, w_dn, w_kr2, w_g, w_qu, w_ku, w_vu = weights

    def whole(a):
        return pl.BlockSpec(a.shape, lambda i: (0,) * a.ndim)

    def mod_spec(k):
        return pl.BlockSpec((None, None, 1, d), lambda i: (i // tiles_per_seq, k, 0, 0))

    def rows(width):
        return pl.BlockSpec((tm, width), lambda i: (i, 0))

    width = MLA_HEADS * MLA_HEAD_LANES
    return pl.pallas_call(
        _in_proj_kernel,
        grid=(n // tm,),
        in_specs=[rows(d), mod_spec(1), mod_spec(0), rows(1), whole(invf),
                  whole(w_sb), whole(w_dn), whole(w_kr2), whole(w_g), whole(qg), whole(kvg),
                  whole(w_qu), whole(w_ku), whole(w_vu)],
        out_specs=[rows(3 * SB_WIDTH), rows(width), rows(width), rows(width), rows(2 * d)],
        out_shape=[jax.ShapeDtypeStruct((n, 3 * SB_WIDTH), BF16),
                   jax.ShapeDtypeStruct((n, width), BF16),
                   jax.ShapeDtypeStruct((n, width), BF16),
                   jax.ShapeDtypeStruct((n, width), BF16),
                   jax.ShapeDtypeStruct((n, 2 * d), F32)],
        compiler_params=_cparams(("arbitrary",)),
        name="in_proj",
    )(x2, mod4, mod4, posf, invf, w_sb, w_dn, w_kr2, w_g, qg, kvg, w_qu, w_ku, w_vu)


def _suffix_sum_matrix():
    j = jnp.arange(2 * LANES)[:, None] % LANES
    n = jnp.arange(2 * LANES)[None, :]
    return jnp.where(n < LANES, (j >= n), True).astype(F32)


def _sb_kernel(q_ref, k_ref, v_ref, tri_ref, o_ref, qh_ref, c_ref, acc_ref, z_ref, w_ref,
               *, tq, tk, n_pairs):
    qi = pl.program_id(2)
    lane = lax.broadcasted_iota(jnp.int32, (tq, LANES), 1)
    row = lax.broadcasted_iota(jnp.int32, (tq, LANES), 0)
    n_sub = tk // LANES
    n_chains = 2 * n_pairs

    for p in range(n_pairs):
        qp = q_ref[:, p * LANES:(p + 1) * LANES]
        for h in range(2):
            head_lanes = (lane < SB_HEAD_DIM) if h == 0 else (lane >= SB_HEAD_DIM)
            qh_ref[2 * p + h] = jnp.where(head_lanes, qp, jnp.zeros_like(qp))
    c_ref[...] = jnp.zeros_like(c_ref)
    acc_ref[...] = jnp.zeros_like(acc_ref)

    def key_rows(t):
        return pl.ds(pl.multiple_of((qi - t) * tk, tk), tk)

    def pair_lanes(ci):
        return slice((ci // 2) * LANES, (ci // 2 + 1) * LANES)

    def valid_keys(s):
        return (lane + s * LANES) < row

    def scores(t, slot, diagonal):
        del diagonal
        for ci in range(n_chains):
            z_ref[slot, ci] = _dot_t(qh_ref[ci], k_ref[key_rows(t), pair_lanes(ci)])

    def weights(t, slot, diagonal):
        del t
        tri = tri_ref[...]
        sign = jnp.uint32(0x80000000)
        bf16_bits = jnp.uint32(0xFFFF0000)
        for ci in range(n_chains):
            c = c_ref[ci]
            for s in reversed(range(n_sub)):
                z = z_ref[slot, ci, :, s * LANES:(s + 1) * LANES]
                neg_abs = pltpu.bitcast(pltpu.bitcast(z, jnp.uint32) | sign, F32)
                sp = jnp.maximum(z, 0.0) + jnp.log(1.0 + jnp.exp2(neg_abs)) * LOG2_E
                if diagonal:
                    sp = jnp.where(valid_keys(s), sp, 0.0)
                hi = pltpu.bitcast(pltpu.bitcast(sp, jnp.uint32) & bf16_bits, F32)
                r = _dot(jnp.concatenate([hi, sp - hi], axis=1), tri)
                w = jnp.exp2(z - (c + r[:, :LANES]))
                if diagonal:
                    w = jnp.where(valid_keys(s), w, 0.0)
                w_ref[slot, ci, :, s * LANES:(s + 1) * LANES] = w.astype(BF16)
                c = c + r[:, LANES:]
            c_ref[ci] = c

    def values(t, slot):
        for ci in range(n_chains):
            acc_ref[ci] += _dot(w_ref[slot, ci], v_ref[key_rows(t), pair_lanes(ci)])

    _pipeline3(qi + 1, scores, weights, values)

    for p in range(n_pairs):
        out = jnp.where(lane < SB_HEAD_DIM, acc_ref[2 * p], acc_ref[2 * p + 1])
        o_ref[:, p * LANES:(p + 1) * LANES] = out.astype(o_ref.dtype)


def _sb_attention(sb3, tri):
    bsz, seq, _ = sb3.shape
    tq = min(ATTN_Q, seq)
    tk = tq
    n_pairs = SB_PAIRS_PER_STEP
    width = n_pairs * LANES
    groups = SB_WIDTH // width
    return pl.pallas_call(
        functools.partial(_sb_kernel, tq=tq, tk=tk, n_pairs=n_pairs),
        grid=(bsz, groups, seq // tq),
        in_specs=[pl.BlockSpec((None, tq, width), lambda b, g, i: (b, i, g)),
                  pl.BlockSpec((None, seq, width), lambda b, g, i: (b, 0, groups + g)),
                  pl.BlockSpec((None, seq, width), lambda b, g, i: (b, 0, 2 * groups + g)),
                  pl.BlockSpec(tri.shape, lambda b, g, i: (0, 0))],
        out_specs=pl.BlockSpec((None, tq, width), lambda b, g, i: (b, i, g)),
        out_shape=jax.ShapeDtypeStruct((bsz, seq, SB_WIDTH), BF16),
        scratch_shapes=[pltpu.VMEM((2 * n_pairs, tq, LANES), BF16),
                        pltpu.VMEM((2 * n_pairs, tq, LANES), F32),
                        pltpu.VMEM((2 * n_pairs, tq, LANES), F32),
                        pltpu.VMEM((2, 2 * n_pairs, tq, tk), F32),
                        pltpu.VMEM((2, 2 * n_pairs, tq, tk), BF16)],
        compiler_params=_cparams(("arbitrary", "arbitrary", "arbitrary")),
        name="sb_attn",
    )(sb3, sb3, sb3, tri)


def _pipeline3(n, stage_a, stage_b, stage_c):
    stage_a(0, 0, True)

    @pl.when(n >= 2)
    def _():
        stage_a(1, 1, False)

    stage_b(0, 0, True)

    def step(i, slot):
        stage_c(i - 2, slot)
        stage_b(i - 1, 1 - slot, False)
        stage_a(i, slot, False)

    def body(k, carry):
        step(2 + 2 * k, 0)
        step(3 + 2 * k, 1)
        return carry

    lax.fori_loop(0, jnp.maximum(n - 2, 0) // 2, body, 0)

    @pl.when((n >= 3) & (n % 2 == 1))
    def _():
        step(n - 1, 0)

    @pl.when(n >= 2)
    def _():
        stage_c(n - 2, n % 2)
        stage_b(n - 1, (n - 1) % 2, False)

    stage_c(n - 1, (n - 1) % 2)


def _mla_kernel(q_ref, k_ref, v_ref, o_ref, m_ref, acc_ref, s_ref, p_ref, alpha_ref, *, tq, tk, n_heads):
    qi = pl.program_id(2)
    lane = lax.broadcasted_iota(jnp.int32, (tq, LANES), 1)
    n_sub = tk // LANES
    m_ref[...] = jnp.full_like(m_ref, -jnp.inf)
    acc_ref[...] = jnp.zeros_like(acc_ref)

    def key_rows(t):
        return pl.ds(pl.multiple_of((qi - t) * tk, tk), tk)

    def head_lanes(ci):
        return slice(ci * MLA_HEAD_LANES, (ci + 1) * MLA_HEAD_LANES)

    def scores(t, slot, diagonal):
        for ci in range(n_heads):
            s = _dot_t(q_ref[:, head_lanes(ci)], k_ref[key_rows(t), head_lanes(ci)])
            if diagonal:
                row = lax.broadcasted_iota(jnp.int32, (tq, tk), 0)
                col = lax.broadcasted_iota(jnp.int32, (tq, tk), 1)
                s = jnp.where(col <= row, s, -jnp.inf)
            s_ref[slot, ci] = s

    def numerators(t, slot, diagonal):
        del t, diagonal
        for ci in range(n_heads):
            s = s_ref[slot, ci]
            m = m_ref[ci]
            m_new = jnp.maximum(m, jnp.broadcast_to(jnp.max(s, axis=-1, keepdims=True), (tq, LANES)))
            alpha_ref[slot, ci] = jnp.exp2(m - m_new)
            for j in range(n_sub):
                sl = slice(j * LANES, (j + 1) * LANES)
                p_ref[slot, ci, :, sl] = jnp.exp2(s[:, sl] - m_new).astype(BF16)
            m_ref[ci] = m_new

    def values(t, slot):
        for ci in range(n_heads):
            pv = _dot(p_ref[slot, ci], v_ref[key_rows(t), head_lanes(ci)])
            acc_ref[ci] = alpha_ref[slot, ci] * acc_ref[ci] + pv

    _pipeline3(qi + 1, scores, numerators, values)

    for p in range(n_heads // 2):
        res = []
        for h in range(2):
            acc = acc_ref[2 * p + h]
            res.append(acc / pltpu.roll(acc, MLA_V_DIM, 1))
        out = jnp.where(lane < MLA_V_DIM, res[0], pltpu.roll(res[1], MLA_V_DIM, 1))
        o_ref[:, p * LANES:(p + 1) * LANES] = out.astype(o_ref.dtype)


def _mla_attention(qm3, km3, vm3):
    bsz, seq, _ = qm3.shape
    tq = min(ATTN_Q, seq)
    tk = tq
    n_heads = MLA_HEADS_PER_STEP
    groups = MLA_HEADS // n_heads
    qk_width = n_heads * MLA_HEAD_LANES
    o_width = n_heads * MLA_V_DIM
    return pl.pallas_call(
        functools.partial(_mla_kernel, tq=tq, tk=tk, n_heads=n_heads),
        grid=(bsz, groups, seq // tq),
        in_specs=[pl.BlockSpec((None, tq, qk_width), lambda b, g, i: (b, i, g)),
                  pl.BlockSpec((None, seq, qk_width), lambda b, g, i: (b, 0, g)),
                  pl.BlockSpec((None, seq, qk_width), lambda b, g, i: (b, 0, g))],
        out_specs=pl.BlockSpec((None, tq, o_width), lambda b, g, i: (b, i, g)),
        out_shape=jax.ShapeDtypeStruct((bsz, seq, MLA_WIDTH), BF16),
        scratch_shapes=[pltpu.VMEM((n_heads, tq, LANES), F32),
                        pltpu.VMEM((n_heads, tq, LANES), F32),
                        pltpu.VMEM((2, n_heads, tq, tk), F32),
                        pltpu.VMEM((2, n_heads, tq, tk), BF16),
                        pltpu.VMEM((2, n_heads, tq, LANES), F32)],
        compiler_params=_cparams(("arbitrary", "arbitrary", "arbitrary")),
        name="mla_attn",
    )(qm3, km3, vm3)


def _layer_norm(v, g, b):
    mu = jnp.mean(v, axis=-1, keepdims=True)
    dv = v - mu
    var = jnp.mean(dv * dv, axis=-1, keepdims=True)
    return dv * lax.rsqrt(var + LN_EPS) * g + b


ROUTER_NO_LANE = 1 << 20


def _router_logits(u2, wr_ref, br_ref):
    return _dot(u2.astype(BF16), wr_ref[...]) + br_ref[...]


def _group_lanes(lane):
    return (lane >= N_EXPERTS) & (lane < N_EXPERTS + N_GROUPS)


def _merge_kernel(osb_ref, omla_ref, g_ref, x_ref, gate1_ref, scale2_ref, shift2_ref,
                  wsb_ref, wmla_ref, wout_ref, ln_g_ref, ln_b_ref, wr_ref, br_ref,
                  x1_ref, u2_ref, gidx_ref):
    d = x_ref.shape[-1]
    y_sb = _dot(osb_ref[...], wsb_ref[...])
    y_mla = _dot(omla_ref[...], wmla_ref[...])
    mixed = jax.nn.sigmoid(g_ref[:, :d]) * y_sb + jax.nn.sigmoid(g_ref[:, d:]) * y_mla
    attn = _dot(mixed.astype(BF16), wout_ref[...])
    x1 = _layer_norm(DEEPNORM_ALPHA * x_ref[...] + gate1_ref[...] * attn, ln_g_ref[...], ln_b_ref[...])
    x1_ref[...] = x1
    u2 = x1 * (1.0 + scale2_ref[...]) + shift2_ref[...]
    u2_ref[...] = u2
    logits = _router_logits(u2, wr_ref, br_ref)
    lane = lax.broadcasted_iota(jnp.int32, logits.shape, 1)
    g_log = jnp.where(_group_lanes(lane), logits, -jnp.inf)
    g_max = jnp.max(g_log, axis=-1, keepdims=True)
    gidx_ref[...] = jnp.min(jnp.where(g_log == g_max, lane, ROUTER_NO_LANE), axis=-1, keepdims=True) - N_EXPERTS


def _merge(o_sb, o_mla, gates, x2, mod4, w_bsb, w_bmla, w_out, ln_g, ln_b, w_r, b_r, seq):
    n, d = x2.shape
    tm = min(MERGE_ROWS, seq)
    tiles_per_seq = seq // tm

    def whole(a):
        return pl.BlockSpec(a.shape, lambda i: (0,) * a.ndim)

    def mod_spec(k):
        return pl.BlockSpec((None, None, 1, d), lambda i: (i // tiles_per_seq, k, 0, 0))

    def rows(width):
        return pl.BlockSpec((tm, width), lambda i: (i, 0))

    return pl.pallas_call(
        _merge_kernel,
        grid=(n // tm,),
        in_specs=[rows(SB_WIDTH), rows(MLA_WIDTH), rows(2 * d), rows(d), mod_spec(2), mod_spec(4), mod_spec(3),
                  whole(w_bsb), whole(w_bmla), whole(w_out), whole(ln_g), whole(ln_b), whole(w_r), whole(b_r)],
        out_specs=[rows(d), rows(d), rows(1)],
        out_shape=[jax.ShapeDtypeStruct((n, d), F32),
                   jax.ShapeDtypeStruct((n, d), F32),
                   jax.ShapeDtypeStruct((n, 1), jnp.int32)],
        compiler_params=_cparams(("arbitrary",)),
        name="merge",
    )(o_sb, o_mla, gates, x2, mod4, mod4, mod4, w_bsb, w_bmla, w_out, ln_g, ln_b, w_r, b_r)


def _route_plan(gidx, tm):
    n = gidx.shape[0]
    max_tiles = n // tm + N_GROUPS
    order = jnp.argsort(gidx, stable=True).astype(jnp.int32)
    counts = jnp.sum(gidx[:, None] == jnp.arange(N_GROUPS, dtype=jnp.int32)[None, :], axis=0).astype(jnp.int32)
    tiles = (counts + tm - 1) // tm
    tile_end = jnp.cumsum(tiles)
    tile_start = tile_end - tiles
    run_start = jnp.cumsum(counts) - counts
    tile_ids = jnp.arange(max_tiles, dtype=jnp.int32)
    n_tiles = tile_end[-1]
    tile_group = jnp.minimum(jnp.sum(tile_ids[:, None] >= tile_end[None, :], axis=1), N_GROUPS - 1).astype(jnp.int32)
    rows_before = (tile_ids - tile_start[tile_group]) * tm
    tile_valid = jnp.clip(counts[tile_group] - rows_before, 0, tm)
    tile_valid = jnp.where(tile_ids < n_tiles, tile_valid, 0).astype(jnp.int32)
    sorted_group = gidx[order]
    slot = tile_start[sorted_group] * tm + (jnp.arange(n, dtype=jnp.int32) - run_start[sorted_group])
    slot_token = jnp.zeros((max_tiles * tm,), jnp.int32).at[slot].set(order)
    return tile_group, tile_valid, n_tiles.reshape(1).astype(jnp.int32), slot_token.reshape(max_tiles, 1, tm)


def _experts_kernel(tile_group_ref, tile_valid_ref, n_tiles_ref, tok_ref, tok_next_ref,
                    u2_hbm, wr_ref, br_ref, wg_ref, wu_ref, wd_ref, y_hbm,
                    xbuf, ybuf, gather_sem, scatter_sem, *, tm):
    i = pl.program_id(0)
    n_tiles = n_tiles_ref[0]
    buf = i % 2

    def gather_row(tok, r, b):
        return pltpu.make_async_copy(u2_hbm.at[pl.ds(tok, 1)], xbuf.at[b, pl.ds(r, 1)], gather_sem.at[b])

    def scatter_row(tok, r):
        return pltpu.make_async_copy(ybuf.at[pl.ds(r, 1)], y_hbm.at[pl.ds(tok, 1)], scatter_sem)

    def start_gather(idx_ref, b):
        def body(r, carry):
            gather_row(idx_ref[0, r], r, b).start()
            return carry
        lax.fori_loop(0, tm, body, 0)

    def wait_scatter(n_rows):
        def body(r, carry):
            scatter_row(0, r).wait()
            return carry
        lax.fori_loop(0, n_rows, body, 0)

    @pl.when(i == 0)
    def _():
        start_gather(tok_ref, 0)

    @pl.when(i < n_tiles)
    def _():
        pltpu.make_async_copy(xbuf.at[buf], xbuf.at[buf], gather_sem.at[buf]).wait()

        @pl.when(i + 1 < n_tiles)
        def _():
            start_gather(tok_next_ref, 1 - buf)

        group = tile_group_ref[i]
        x = xbuf[buf]
        logits = _router_logits(x, wr_ref, br_ref)
        lane = lax.broadcasted_iota(jnp.int32, logits.shape, 1)
        neg = -jnp.inf
        g_log = jnp.where(_group_lanes(lane), logits, neg)
        g_max = jnp.max(g_log, axis=-1, keepdims=True)
        p_group = 1.0 / jnp.sum(jnp.exp(g_log - g_max), axis=-1, keepdims=True)
        first = group * EXPERTS_PER_GROUP
        e_log = jnp.where((lane >= first) & (lane < first + EXPERTS_PER_GROUP), logits, neg)
        v1 = jnp.max(e_log, axis=-1, keepdims=True)
        i1 = jnp.min(jnp.where(e_log == v1, lane, ROUTER_NO_LANE), axis=-1, keepdims=True)
        e_log2 = jnp.where(lane == i1, neg, e_log)
        v2 = jnp.max(e_log2, axis=-1, keepdims=True)
        i2 = jnp.min(jnp.where(e_log2 == v2, lane, ROUTER_NO_LANE), axis=-1, keepdims=True)
        e2 = jnp.exp(v2 - v1)
        w1 = p_group / (1.0 + e2)
        w2 = p_group * e2 / (1.0 + e2)
        combine = jnp.where(lane == i1, w1, 0.0) + jnp.where(lane == i2, w2, 0.0)

        xb = x.astype(BF16)
        hidden = []
        for e in range(EXPERTS_PER_GROUP):
            w_tok = jnp.sum(jnp.where(lane == first + e, combine, 0.0), axis=-1, keepdims=True)
            hg = _dot(xb, wg_ref[e])
            hu = _dot(xb, wu_ref[e])
            hidden.append((hg * jax.nn.sigmoid(hg) * hu * w_tok).astype(BF16))
        y = _dot(jnp.concatenate(hidden, axis=1), wd_ref[...])

        @pl.when(i >= 1)
        def _():
            wait_scatter(tile_valid_ref[jnp.maximum(i - 1, 0)])

        ybuf[...] = y
        n_valid = tile_valid_ref[i]

        def scatter_body(r, carry):
            scatter_row(tok_ref[0, r], r).start()
            return carry
        lax.fori_loop(0, n_valid, scatter_body, 0)

        @pl.when(i == n_tiles - 1)
        def _():
            wait_scatter(n_valid)


def _experts(u2, gidx, w_r, b_r, w_g, w_u, w_d):
    n, d = u2.shape
    tm = MOE_ROWS
    d_e = w_g.shape[-1]
    tile_group, tile_valid, n_tiles, slot_token = _route_plan(gidx, tm)
    max_tiles = slot_token.shape[0]
    w_g = w_g.reshape(N_GROUPS, EXPERTS_PER_GROUP, d, d_e)
    w_u = w_u.reshape(N_GROUPS, EXPERTS_PER_GROUP, d, d_e)
    w_d = w_d.reshape(N_GROUPS, EXPERTS_PER_GROUP * d_e, d)
    grid_spec = pltpu.PrefetchScalarGridSpec(
        num_scalar_prefetch=3,
        grid=(max_tiles,),
        in_specs=[pl.BlockSpec((None, 1, tm), lambda i, tg, tv, nt: (i, 0, 0), memory_space=pltpu.SMEM),
                  pl.BlockSpec((None, 1, tm), lambda i, tg, tv, nt: (jnp.minimum(i + 1, max_tiles - 1), 0, 0),
                               memory_space=pltpu.SMEM),
                  pl.BlockSpec(memory_space=pl.ANY),
                  pl.BlockSpec(w_r.shape, lambda i, tg, tv, nt: (0, 0)),
                  pl.BlockSpec(b_r.shape, lambda i, tg, tv, nt: (0, 0)),
                  pl.BlockSpec((None, EXPERTS_PER_GROUP, d, d_e), lambda i, tg, tv, nt: (tg[i], 0, 0, 0)),
                  pl.BlockSpec((None, EXPERTS_PER_GROUP, d, d_e), lambda i, tg, tv, nt: (tg[i], 0, 0, 0)),
                  pl.BlockSpec((None, EXPERTS_PER_GROUP * d_e, d), lambda i, tg, tv, nt: (tg[i], 0, 0))],
        out_specs=pl.BlockSpec(memory_space=pl.ANY),
        scratch_shapes=[pltpu.VMEM((2, tm, d), F32),
                        pltpu.VMEM((tm, d), F32),
                        pltpu.SemaphoreType.DMA((2,)),
                        pltpu.SemaphoreType.DMA(())],
    )
    return pl.pallas_call(
        functools.partial(_experts_kernel, tm=tm),
        grid_spec=grid_spec,
        out_shape=jax.ShapeDtypeStruct((n, d), F32),
        compiler_params=_cparams(("arbitrary",)),
        name="experts",
    )(tile_group, tile_valid, n_tiles, slot_token, slot_token, u2, w_r, b_r, w_g, w_u, w_d)


def _ln2_kernel(x1_ref, y_ref, gate2_ref, ln_g_ref, ln_b_ref, o_ref):
    v = DEEPNORM_ALPHA * x1_ref[...] + gate2_ref[...] * y_ref[...]
    o_ref[...] = _layer_norm(v, ln_g_ref[...], ln_b_ref[...])


def _ln2(x1, y, mod4, ln_g, ln_b, seq):
    n, d = x1.shape
    tm = min(LN2_ROWS, seq)
    tiles_per_seq = seq // tm
    rows = pl.BlockSpec((tm, d), lambda i: (i, 0))
    vec = pl.BlockSpec((1, d), lambda i: (0, 0))
    return pl.pallas_call(
        _ln2_kernel,
        grid=(n // tm,),
        in_specs=[rows, rows, pl.BlockSpec((None, None, 1, d), lambda i: (i // tiles_per_seq, 5, 0, 0)), vec, vec],
        out_specs=rows,
        out_shape=jax.ShapeDtypeStruct((n, d), F32),
        compiler_params=_cparams(("arbitrary",)),
        name="ln2",
    )(x1, y, mod4, ln_g, ln_b)


def kernel(x, c, positions, w_ada, b_ada, w_in, mla_q_norm_g, w_q_up, mla_kv_norm_g, w_kv_up, w_branch_sb, w_branch_mla, w_out, ln1_g, ln1_b, w_router_group, b_router_group, w_router_expert, b_router_expert, w_exp_gate, w_exp_up, w_exp_down, ln2_g, ln2_b):
    bsz, seq, d = x.shape
    n = bsz * seq
    invf = _rope_inv_freq_tile()
    posf = positions.astype(F32).reshape(n, 1)
    tri = _suffix_sum_matrix()
    for l in range(w_ada.shape[0]):
        x2 = x.reshape(n, d)
        mod4 = _adaln_mod(c, w_ada[l], b_ada[l]).reshape(bsz, N_MOD, 1, d)
        weights = _prep_in_weights(w_in[l], w_q_up[l], w_kv_up[l])
        sb, qm, km, vm, gates = _in_proj(x2, mod4, posf, invf, weights,
                                         mla_q_norm_g[l].reshape(1, -1), mla_kv_norm_g[l].reshape(1, -1), seq)
        o_sb = _sb_attention(sb.reshape(bsz, seq, -1), tri).reshape(n, SB_WIDTH)
        o_mla = _mla_attention(qm.reshape(bsz, seq, -1), km.reshape(bsz, seq, -1),
                               vm.reshape(bsz, seq, -1)).reshape(n, MLA_WIDTH)
        w_r = jnp.zeros((d, LANES), F32)
        w_r = w_r.at[:, :N_EXPERTS].set(w_router_expert[l]).at[:, N_EXPERTS:N_EXPERTS + N_GROUPS].set(w_router_group[l])
        b_r = jnp.zeros((1, LANES), F32)
        b_r = b_r.at[0, :N_EXPERTS].set(b_router_expert[l]).at[0, N_EXPERTS:N_EXPERTS + N_GROUPS].set(b_router_group[l])
        w_r = w_r.astype(BF16)
        x1, u2, gidx = _merge(o_sb, o_mla, gates, x2, mod4,
                              w_branch_sb[l].astype(BF16), w_branch_mla[l].astype(BF16), w_out[l].astype(BF16),
                              ln1_g[l].reshape(1, d), ln1_b[l].reshape(1, d), w_r, b_r, seq)
        y = _experts(u2, gidx.reshape(n), w_r, b_r, w_exp_gate[l].astype(BF16), w_exp_up[l].astype(BF16),
                     w_exp_down[l].astype(BF16))
        out = _ln2(x1, y, mod4, ln2_g[l].reshape(1, d), ln2_b[l].reshape(1, d), seq)
        x = out.reshape(bsz, seq, d)
    return x
```

```python
import functools
import math

import jax
import jax.numpy as jnp
import numpy as np
from jax import lax
from jax.experimental import pallas as pl
from jax.experimental.pallas import tpu as pltpu

F32 = jnp.float32
BF16 = jnp.bfloat16

D_MODEL = 1024
SB_HEADS = 8
SB_HEAD_DIM = 64
SB_WIDTH = SB_HEADS * SB_HEAD_DIM
MLA_HEADS = 8
MLA_NOPE_DIM = 64
MLA_ROPE_DIM = 32
MLA_V_DIM = 64
MLA_Q_RANK = 384
MLA_KV_RANK = 256
MLA_QK_DIM = MLA_NOPE_DIM + MLA_ROPE_DIM
MLA_WIDTH = MLA_HEADS * MLA_V_DIM
ROPE_THETA = 10000.0
N_GROUPS = 4
EXPERTS_PER_GROUP = 8
N_EXPERTS = N_GROUPS * EXPERTS_PER_GROUP
D_EXPERT = 256
DEPTH = 1
DEEPNORM_ALPHA = (2.0 * DEPTH) ** 0.25
LN_EPS = 1e-5
RMS_EPS = 1e-6
LOG2_E = math.log2(math.e)
MASK_BIAS = -1e30
N_MOD = 6

LANES = 128
MLA_HEAD_LANES = 128
VMEM_LIMIT_BYTES = 56 * 1024 * 1024

PROJ_ROWS = 256
ATTN_Q = 256
ATTN_K = 256
MERGE_ROWS = 256
MOE_ROWS = 512
LN2_ROWS = 512
DMA_ISSUE_UNROLL = 8
SB_PAIRS_PER_STEP = 2
MLA_HEADS_PER_STEP = 4


def _cparams(sem):
    return pltpu.CompilerParams(dimension_semantics=sem, vmem_limit_bytes=VMEM_LIMIT_BYTES)


def _dot(a, b):
    return jnp.dot(a, b, preferred_element_type=F32)


def _dot_t(a, b):
    return lax.dot_general(a, b, (((1,), (1,)), ((), ())), preferred_element_type=F32)


def _adaln_kernel(c_ref, w_ref, b_ref, o_ref):
    c = c_ref[...]
    ca = c * jax.nn.sigmoid(c)
    o_ref[...] = _dot(ca.astype(BF16), w_ref[...].astype(BF16)) + b_ref[...]


def _adaln_mod(c, w_ada, b_ada):
    bsz, d = c.shape
    return pl.pallas_call(
        _adaln_kernel,
        grid=(N_MOD,),
        in_specs=[pl.BlockSpec((bsz, d), lambda j: (0, 0)),
                  pl.BlockSpec((d, d), lambda j: (0, j)),
                  pl.BlockSpec((1, d), lambda j: (0, j))],
        out_specs=pl.BlockSpec((bsz, d), lambda j: (0, j)),
        out_shape=jax.ShapeDtypeStruct((bsz, N_MOD * d), F32),
        compiler_params=_cparams(("arbitrary",)),
        name="adaln_mod",
    )(c, w_ada, b_ada.reshape(1, N_MOD * d))


def _in_proj_kernel(x_ref, scale_ref, shift_ref, pos_ref, invf_ref,
                    w_sb_ref, w_dn_ref, w_kr_ref, w_g_ref, qg_ref, kvg_ref,
                    w_qu_ref, w_ku_ref, w_vu_ref,
                    sb_ref, qm_ref, km_ref, vm_ref, g_ref):
    u = (x_ref[...] * (1.0 + scale_ref[...]) + shift_ref[...]).astype(BF16)

    sb = _dot(u, w_sb_ref[...])
    sb_scale = LOG2_E / math.sqrt(SB_HEAD_DIM)
    sb_ref[:, :SB_WIDTH] = (sb[:, :SB_WIDTH] * sb_scale).astype(BF16)
    sb_ref[:, SB_WIDTH:] = sb[:, SB_WIDTH:].astype(BF16)

    g_ref[...] = _dot(u, w_g_ref[...])

    ang = pos_ref[...] * invf_ref[...]
    cos = jnp.cos(ang)
    sin = jnp.sin(ang)

    dn = _dot(u, w_dn_ref[...])
    q_dn = dn[:, :MLA_Q_RANK]
    kv_dn = dn[:, MLA_Q_RANK:]
    qn = q_dn * lax.rsqrt(jnp.mean(q_dn * q_dn, axis=-1, keepdims=True) + RMS_EPS) * qg_ref[...]
    kvn = kv_dn * lax.rsqrt(jnp.mean(kv_dn * kv_dn, axis=-1, keepdims=True) + RMS_EPS) * kvg_ref[...]
    qn = qn.astype(BF16)
    kvn = kvn.astype(BF16)

    width = MLA_HEADS * MLA_HEAD_LANES
    qu = _dot(qn, w_qu_ref[...])
    ku = _dot(kvn, w_ku_ref[...])
    vu = _dot(kvn, w_vu_ref[...])
    v_lane = lax.broadcasted_iota(jnp.int32, vu.shape, 1) % MLA_HEAD_LANES
    vm_ref[...] = jnp.where(v_lane < MLA_V_DIM, vu, 1.0).astype(BF16)
    kr = _dot(u, w_kr_ref[...])
    kr = kr[:, :LANES] * cos + kr[:, LANES:] * sin
    mla_scale = LOG2_E / math.sqrt(MLA_QK_DIM)
    for h in range(MLA_HEADS):
        sl = slice(h * MLA_HEAD_LANES, (h + 1) * MLA_HEAD_LANES)
        sl_rot = slice(width + h * MLA_HEAD_LANES, width + (h + 1) * MLA_HEAD_LANES)
        qm_ref[:, sl] = ((qu[:, sl] * cos + qu[:, sl_rot] * sin) * mla_scale).astype(BF16)
        km_ref[:, sl] = (ku[:, sl] + kr).astype(BF16)


def _rope_inv_freq_tile():
    inv_freq = 1.0 / (ROPE_THETA ** (jnp.arange(0, MLA_ROPE_DIM, 2, dtype=F32) / MLA_ROPE_DIM))
    tile = jnp.zeros((LANES,), F32)
    half = MLA_ROPE_DIM // 2
    tile = tile.at[MLA_NOPE_DIM:MLA_NOPE_DIM + half].set(inv_freq)
    tile = tile.at[MLA_NOPE_DIM + half:MLA_NOPE_DIM + 2 * half].set(inv_freq)
    return tile.reshape(1, LANES)


def _rotate_half_cols(w):
    half = MLA_ROPE_DIM // 2
    return jnp.concatenate([-w[..., half:], w[..., :half]], axis=-1)


def _prep_in_weights(w_in, w_q_up, w_kv_up):
    o = 0
    w_sb = w_in[:, o:o + 3 * SB_WIDTH]; o += 3 * SB_WIDTH
    w_dn = w_in[:, o:o + MLA_Q_RANK + MLA_KV_RANK]; o += MLA_Q_RANK + MLA_KV_RANK
    w_kr = w_in[:, o:o + MLA_ROPE_DIM]; o += MLA_ROPE_DIM
    w_g = w_in[:, o:]

    def rope_tile(w):
        z = jnp.zeros((w.shape[0], LANES), w.dtype)
        return z.at[:, MLA_NOPE_DIM:MLA_NOPE_DIM + MLA_ROPE_DIM].set(w)

    w_kr2 = jnp.concatenate([rope_tile(w_kr), rope_tile(_rotate_half_cols(w_kr))], axis=1)

    wq = w_q_up.reshape(MLA_Q_RANK, MLA_HEADS, MLA_QK_DIM)
    q_base = jnp.zeros((MLA_Q_RANK, MLA_HEADS, MLA_HEAD_LANES), w_q_up.dtype).at[:, :, :MLA_QK_DIM].set(wq)
    q_rot = jnp.zeros((MLA_Q_RANK, MLA_HEADS, MLA_HEAD_LANES), w_q_up.dtype)
    q_rot = q_rot.at[:, :, MLA_NOPE_DIM:MLA_QK_DIM].set(_rotate_half_cols(wq[:, :, MLA_NOPE_DIM:]))
    w_qu = jnp.concatenate([q_base.reshape(MLA_Q_RANK, -1), q_rot.reshape(MLA_Q_RANK, -1)], axis=1)

    wkv = w_kv_up.reshape(MLA_KV_RANK, MLA_HEADS, MLA_NOPE_DIM + MLA_V_DIM)
    w_ku = jnp.zeros((MLA_KV_RANK, MLA_HEADS, MLA_HEAD_LANES), w_kv_up.dtype)
    w_ku = w_ku.at[:, :, :MLA_NOPE_DIM].set(wkv[:, :, :MLA_NOPE_DIM]).reshape(MLA_KV_RANK, -1)
    w_vu = jnp.zeros((MLA_KV_RANK, MLA_HEADS, MLA_HEAD_LANES), w_kv_up.dtype)
    w_vu = w_vu.at[:, :, :MLA_V_DIM].set(wkv[:, :, MLA_NOPE_DIM:]).reshape(MLA_KV_RANK, -1)
    return tuple(w.astype(BF16) for w in (w_sb, w_dn, w_kr2, w_g, w_qu, w_ku, w_vu))


def _in_proj(x2, mod4, posf, invf, weights, qg, kvg, seq):
    n, d = x2.shape
    tm = min(PROJ_ROWS, seq)
    tiles_per_seq = seq // tm
    w_sb, w_dn, w_kr2, w_g, w_qu, w_ku, w_vu = weights

    def whole(a):
        return pl.BlockSpec(a.shape, lambda i: (0,) * a.ndim)

    def mod_spec(k):
        return pl.BlockSpec((None, None, 1, d), lambda i: (i // tiles_per_seq, k, 0, 0))

    def rows(width):
        return pl.BlockSpec((tm, width), lambda i: (i, 0))

    width = MLA_HEADS * MLA_HEAD_LANES
    return pl.pallas_call(
        _in_proj_kernel,
        grid=(n // tm,),
        in_specs=[rows(d), mod_spec(1), mod_spec(0), rows(1), whole(invf),
                  whole(w_sb), whole(w_dn), whole(w_kr2), whole(w_g), whole(qg), whole(kvg),
                  whole(w_qu), whole(w_ku), whole(w_vu)],
        out_specs=[rows(3 * SB_WIDTH), rows(width), rows(width), rows(width), rows(2 * d)],
        out_shape=[jax.ShapeDtypeStruct((n, 3 * SB_WIDTH), BF16),
                   jax.ShapeDtypeStruct((n, width), BF16),
                   jax.ShapeDtypeStruct((n, width), BF16),
                   jax.ShapeDtypeStruct((n, width), BF16),
                   jax.ShapeDtypeStruct((n, 2 * d), F32)],
        compiler_params=_cparams(("arbitrary",)),
        name="in_proj",
    )(x2, mod4, mod4, posf, invf, w_sb, w_dn, w_kr2, w_g, qg, kvg, w_qu, w_ku, w_vu)


def _item_schedule(n_q):
    q_of, t_of = [], []
    for qi in range(n_q):
        for t in range(qi + 1):
            q_of.append(qi)
            t_of.append(t)
    return np.asarray(q_of, np.int32), np.asarray(t_of, np.int32)


def _causal_bias(tq, tk, strict):
    r = jnp.arange(tq)[:, None]
    c = jnp.arange(tk)[None, :]
    visible = (c < r) if strict else (c <= r)
    return jnp.stack([jnp.zeros((tq, tk), F32), jnp.where(visible, 0.0, MASK_BIAS).astype(F32)])


def _pipeline3(n, stage_a, stage_b, stage_c):
    stage_a(0, 0)
    if n >= 2:
        stage_a(1, 1)
    stage_b(0, 0)

    def step(i, slot):
        stage_c(i - 2, slot)
        stage_b(i - 1, 1 - slot)
        stage_a(i, slot)

    def body(k, carry):
        step(2 + 2 * k, 0)
        step(3 + 2 * k, 1)
        return carry

    pairs = max(n - 2, 0) // 2
    lax.fori_loop(0, pairs, body, 0)
    if n >= 3 and n % 2 == 1:
        step(n - 1, 0)
    if n >= 2:
        stage_c(n - 2, n % 2)
        stage_b(n - 1, (n - 1) % 2)
    stage_c(n - 1, (n - 1) % 2)


def _suffix_sum_matrix():
    j = jnp.arange(2 * LANES)[:, None] % LANES
    n = jnp.arange(2 * LANES)[None, :]
    return jnp.where(n < LANES, (j >= n), True).astype(F32)


def _sb_kernel(item_q_ref, item_t_ref, q_ref, k_ref, v_ref, tri_ref, bias_ref, o_ref,
               qh_ref, c_ref, acc_ref, z_ref, w_ref, *, tq, tk, n_pairs, n_items):
    lane = lax.broadcasted_iota(jnp.int32, (tq, LANES), 1)
    n_sub = tk // LANES
    n_chains = 2 * n_pairs

    for p in range(n_pairs):
        qp = q_ref[:, p * LANES:(p + 1) * LANES]
        q_lane = lax.broadcasted_iota(jnp.int32, qp.shape, 1)
        for h in range(2):
            head_lanes = (q_lane < SB_HEAD_DIM) if h == 0 else (q_lane >= SB_HEAD_DIM)
            qh_ref[2 * p + h] = jnp.where(head_lanes, qp, jnp.zeros_like(qp))
    c_ref[...] = jnp.zeros_like(c_ref)
    acc_ref[...] = jnp.zeros_like(acc_ref)

    def item(f):
        qi = item_q_ref[f]
        t = item_t_ref[f]
        q_rows = pl.ds(pl.multiple_of(qi * tq, tq), tq)
        k_rows = pl.ds(pl.multiple_of((qi - t) * tk, tk), tk)
        return q_rows, k_rows, t

    def pair_lanes(ci):
        return slice((ci // 2) * LANES, (ci // 2 + 1) * LANES)

    def scores(f, slot):
        q_rows, k_rows, t = item(f)
        bias = bias_ref[(t == 0).astype(jnp.int32)]
        for ci in range(n_chains):
            z_ref[slot, ci] = _dot_t(qh_ref[ci, q_rows, :], k_ref[k_rows, pair_lanes(ci)]) + bias

    def weights(f, slot):
        _, _, t = item(f)
        keep = jnp.where(t == 0, 0.0, 1.0)
        tri = tri_ref[...]
        sign = jnp.uint32(0x80000000)
        bf16_bits = jnp.uint32(0xFFFF0000)
        for ci in range(n_chains):
            c = c_ref[ci] * keep
            for s in reversed(range(n_sub)):
                z = z_ref[slot, ci, :, s * LANES:(s + 1) * LANES]
                neg_abs = pltpu.bitcast(pltpu.bitcast(z, jnp.uint32) | sign, F32)
                sp = jnp.maximum(z, 0.0) + jnp.log2(1.0 + jnp.exp2(neg_abs))
                hi = pltpu.bitcast(pltpu.bitcast(sp, jnp.uint32) & bf16_bits, F32)
                r = _dot(jnp.concatenate([hi, sp - hi], axis=1), tri)
                w_ref[slot, ci, :, s * LANES:(s + 1) * LANES] = jnp.exp2(z - (c + r[:, :LANES])).astype(BF16)
                c = c + r[:, LANES:]
            c_ref[ci] = c

    def values(f, slot):
        q_rows, k_rows, t = item(f)
        keep = jnp.where(t == 0, 0.0, 1.0)
        for p in range(n_pairs):
            accs = []
            for ci in (2 * p, 2 * p + 1):
                acc = acc_ref[ci] * keep + _dot(w_ref[slot, ci], v_ref[k_rows, pair_lanes(ci)])
                acc_ref[ci] = acc
                accs.append(acc)
            o_ref[q_rows, p * LANES:(p + 1) * LANES] = jnp.where(lane < SB_HEAD_DIM, accs[0], accs[1]).astype(o_ref.dtype)

    _pipeline3(n_items, scores, weights, values)


def _sb_attention(sb3, tri):
    bsz, seq, _ = sb3.shape
    tq = min(ATTN_Q, seq)
    tk = tq
    n_pairs = SB_PAIRS_PER_STEP
    width = n_pairs * LANES
    groups = SB_WIDTH // width
    item_q, item_t = _item_schedule(seq // tq)
    bias = _causal_bias(tq, tk, strict=True)

    def seq_block(first_block):
        return pl.BlockSpec((None, seq, width), lambda b, g, iq, it: (b, 0, first_block + g))

    grid_spec = pltpu.PrefetchScalarGridSpec(
        num_scalar_prefetch=2,
        grid=(bsz, groups),
        in_specs=[seq_block(0), seq_block(groups), seq_block(2 * groups),
                  pl.BlockSpec(tri.shape, lambda b, g, iq, it: (0, 0)),
                  pl.BlockSpec(bias.shape, lambda b, g, iq, it: (0, 0, 0))],
        out_specs=seq_block(0),
        scratch_shapes=[pltpu.VMEM((2 * n_pairs, seq, LANES), BF16),
                        pltpu.VMEM((2 * n_pairs, tq, LANES), F32),
                        pltpu.VMEM((2 * n_pairs, tq, LANES), F32),
                        pltpu.VMEM((2, 2 * n_pairs, tq, tk), F32),
                        pltpu.VMEM((2, 2 * n_pairs, tq, tk), BF16)],
    )
    return pl.pallas_call(
        functools.partial(_sb_kernel, tq=tq, tk=tk, n_pairs=n_pairs, n_items=len(item_q)),
        grid_spec=grid_spec,
        out_shape=jax.ShapeDtypeStruct((bsz, seq, SB_WIDTH), BF16),
        compiler_params=_cparams(("arbitrary", "arbitrary")),
        name="sb_attn",
    )(item_q, item_t, sb3, sb3, sb3, tri, bias)


def _mla_kernel(item_q_ref, item_t_ref, q_ref, k_ref, v_ref, bias_ref, o_ref,
                m_ref, acc_ref, s_ref, p_ref, alpha_ref, *, tq, tk, n_heads, n_items):
    n_sub = tk // LANES
    m_ref[...] = jnp.zeros_like(m_ref)
    acc_ref[...] = jnp.zeros_like(acc_ref)

    def item(f):
        qi = item_q_ref[f]
        t = item_t_ref[f]
        q_rows = pl.ds(pl.multiple_of(qi * tq, tq), tq)
        k_rows = pl.ds(pl.multiple_of((qi - t) * tk, tk), tk)
        return q_rows, k_rows, t

    def head_lanes(ci):
        return slice(ci * MLA_HEAD_LANES, (ci + 1) * MLA_HEAD_LANES)

    def scores(f, slot):
        q_rows, k_rows, t = item(f)
        bias = bias_ref[(t == 0).astype(jnp.int32)]
        for ci in range(n_heads):
            s_ref[slot, ci] = _dot_t(q_ref[q_rows, head_lanes(ci)], k_ref[k_rows, head_lanes(ci)]) + bias

    def numerators(f, slot):
        _, _, t = item(f)
        for ci in range(n_heads):
            s = s_ref[slot, ci]
            m = jnp.where(t == 0, -jnp.inf, m_ref[ci])
            m_new = jnp.maximum(m, jnp.broadcast_to(jnp.max(s, axis=-1, keepdims=True), (tq, LANES)))
            alpha_ref[slot, ci] = jnp.exp2(m - m_new)
            for j in range(n_sub):
                sl = slice(j * LANES, (j + 1) * LANES)
                p_ref[slot, ci, :, sl] = jnp.exp2(s[:, sl] - m_new).astype(BF16)
            m_ref[ci] = m_new

    def values(f, slot):
        q_rows, k_rows, _ = item(f)
        for ci in range(n_heads):
            acc = alpha_ref[slot, ci] * acc_ref[ci] + _dot(p_ref[slot, ci], v_ref[k_rows, head_lanes(ci)])
            acc_ref[ci] = acc
            o_ref[q_rows, head_lanes(ci)] = acc

    _pipeline3(n_items, scores, numerators, values)


def _mla_attention(qm3, km3, vm3):
    bsz, seq, _ = qm3.shape
    tq = min(ATTN_Q, seq)
    tk = tq
    n_heads = MLA_HEADS_PER_STEP
    groups = MLA_HEADS // n_heads
    width = n_heads * MLA_HEAD_LANES
    item_q, item_t = _item_schedule(seq // tq)
    bias = _causal_bias(tq, tk, strict=False)
    seq_block = pl.BlockSpec((None, seq, width), lambda b, g, iq, it: (b, 0, g))
    grid_spec = pltpu.PrefetchScalarGridSpec(
        num_scalar_prefetch=2,
        grid=(bsz, groups),
        in_specs=[seq_block, seq_block, seq_block, pl.BlockSpec(bias.shape, lambda b, g, iq, it: (0, 0, 0))],
        out_specs=seq_block,
        scratch_shapes=[pltpu.VMEM((n_heads, tq, LANES), F32),
                        pltpu.VMEM((n_heads, tq, LANES), F32),
                        pltpu.VMEM((2, n_heads, tq, tk), F32),
                        pltpu.VMEM((2, n_heads, tq, tk), BF16),
                        pltpu.VMEM((2, n_heads, tq, LANES), F32)],
    )
    return pl.pallas_call(
        functools.partial(_mla_kernel, tq=tq, tk=tk, n_heads=n_heads, n_items=len(item_q)),
        grid_spec=grid_spec,
        out_shape=jax.ShapeDtypeStruct((bsz, seq, MLA_HEADS * MLA_HEAD_LANES), F32),
        compiler_params=_cparams(("arbitrary", "arbitrary")),
        name="mla_attn",
    )(item_q, item_t, qm3, km3, vm3, bias)


def _layer_norm(v, g, b):
    mu = jnp.mean(v, axis=-1, keepdims=True)
    dv = v - mu
    var = jnp.mean(dv * dv, axis=-1, keepdims=True)
    return dv * lax.rsqrt(var + LN_EPS) * g + b


ROUTER_NO_LANE = 1 << 20


def _router_logits(u2, wr_ref, br_ref):
    return _dot(u2.astype(BF16), wr_ref[...]) + br_ref[...]


def _group_lanes(lane):
    return (lane >= N_EXPERTS) & (lane < N_EXPERTS + N_GROUPS)


def _merge_kernel(osb_ref, omla_ref, g_ref, x_ref, gate1_ref, scale2_ref, shift2_ref,
                  wsb_ref, wmla_ref, wout_ref, ln_g_ref, ln_b_ref, wr_ref, br_ref,
                  x1_ref, u2_ref, gidx_ref):
    d = x_ref.shape[-1]
    y_sb = _dot(osb_ref[...], wsb_ref[...])
    lane = lax.broadcasted_iota(jnp.int32, (osb_ref.shape[0], LANES), 1)
    o_mla = []
    for p in range(MLA_HEADS // 2):
        res = []
        for h in range(2):
            acc = omla_ref[:, (2 * p + h) * MLA_HEAD_LANES:(2 * p + h + 1) * MLA_HEAD_LANES]
            res.append(acc / pltpu.roll(acc, MLA_V_DIM, 1))
        o_mla.append(jnp.where(lane < MLA_V_DIM, res[0], pltpu.roll(res[1], MLA_V_DIM, 1)).astype(BF16))
    y_mla = _dot(jnp.concatenate(o_mla, axis=1), wmla_ref[...])
    mixed = jax.nn.sigmoid(g_ref[:, :d]) * y_sb + jax.nn.sigmoid(g_ref[:, d:]) * y_mla
    attn = _dot(mixed.astype(BF16), wout_ref[...])
    x1 = _layer_norm(DEEPNORM_ALPHA * x_ref[...] + gate1_ref[...] * attn, ln_g_ref[...], ln_b_ref[...])
    x1_ref[...] = x1
    u2 = x1 * (1.0 + scale2_ref[...]) + shift2_ref[...]
    u2_ref[...] = u2
    logits = _router_logits(u2, wr_ref, br_ref)
    lane = lax.broadcasted_iota(jnp.int32, logits.shape, 1)
    g_log = jnp.where(_group_lanes(lane), logits, -jnp.inf)
    g_max = jnp.max(g_log, axis=-1, keepdims=True)
    gidx_ref[...] = jnp.min(jnp.where(g_log == g_max, lane, ROUTER_NO_LANE), axis=-1, keepdims=True) - N_EXPERTS


def _merge(o_sb, o_mla, gates, x2, mod4, w_bsb, w_bmla, w_out, ln_g, ln_b, w_r, b_r, seq):
    n, d = x2.shape
    tm = min(MERGE_ROWS, seq)
    tiles_per_seq = seq // tm

    def whole(a):
        return pl.BlockSpec(a.shape, lambda i: (0,) * a.ndim)

    def mod_spec(k):
        return pl.BlockSpec((None, None, 1, d), lambda i: (i // tiles_per_seq, k, 0, 0))

    def rows(width):
        return pl.BlockSpec((tm, width), lambda i: (i, 0))

    return pl.pallas_call(
        _merge_kernel,
        grid=(n // tm,),
        in_specs=[rows(SB_WIDTH), rows(MLA_HEADS * MLA_HEAD_LANES), rows(2 * d), rows(d),
                  mod_spec(2), mod_spec(4), mod_spec(3),
                  whole(w_bsb), whole(w_bmla), whole(w_out), whole(ln_g), whole(ln_b), whole(w_r), whole(b_r)],
        out_specs=[rows(d), rows(d), rows(1)],
        out_shape=[jax.ShapeDtypeStruct((n, d), F32),
                   jax.ShapeDtypeStruct((n, d), F32),
                   jax.ShapeDtypeStruct((n, 1), jnp.int32)],
        compiler_params=_cparams(("arbitrary",)),
        name="merge",
    )(o_sb, o_mla, gates, x2, mod4, mod4, mod4, w_bsb, w_bmla, w_out, ln_g, ln_b, w_r, b_r)


def _route_plan(gidx, tm):
    n = gidx.shape[0]
    max_tiles = n // tm + N_GROUPS
    order = jnp.argsort(gidx, stable=True).astype(jnp.int32)
    counts = jnp.sum(gidx[:, None] == jnp.arange(N_GROUPS, dtype=jnp.int32)[None, :], axis=0).astype(jnp.int32)
    tiles = (counts + tm - 1) // tm
    tile_end = jnp.cumsum(tiles)
    tile_start = tile_end - tiles
    run_start = jnp.cumsum(counts) - counts
    tile_ids = jnp.arange(max_tiles, dtype=jnp.int32)
    n_tiles = tile_end[-1]
    tile_group = jnp.minimum(jnp.sum(tile_ids[:, None] >= tile_end[None, :], axis=1), N_GROUPS - 1).astype(jnp.int32)
    rows_before = (tile_ids - tile_start[tile_group]) * tm
    tile_valid = jnp.clip(counts[tile_group] - rows_before, 0, tm)
    tile_valid = jnp.where(tile_ids < n_tiles, tile_valid, 0).astype(jnp.int32)
    row = jnp.arange(tm, dtype=jnp.int32)[None, :]
    sorted_pos = (run_start[tile_group] + rows_before)[:, None] + row
    slot_token = jnp.where(row < tile_valid[:, None], order[jnp.clip(sorted_pos, 0, n - 1)], 0)
    return tile_group, tile_valid, n_tiles.reshape(1).astype(jnp.int32), slot_token.reshape(max_tiles, 1, tm)


def _experts_kernel(tile_group_ref, tile_valid_ref, n_tiles_ref, tok_ref, tok_next_ref,
                    u2_hbm, wr_ref, br_ref, wg_ref, wu_ref, wd_ref, y_hbm,
                    xbuf, ybuf, gather_sem, scatter_sem, *, tm):
    i = pl.program_id(0)
    n_tiles = n_tiles_ref[0]
    buf = i % 2

    def gather_row(tok, r, b):
        return pltpu.make_async_copy(u2_hbm.at[pl.ds(tok, 1)], xbuf.at[b, pl.ds(r, 1)], gather_sem.at[b])

    def scatter_row(tok, r):
        return pltpu.make_async_copy(ybuf.at[pl.ds(r, 1)], y_hbm.at[pl.ds(tok, 1)], scatter_sem)

    def start_gather(idx_ref, b):
        def body(r, carry):
            gather_row(idx_ref[0, r], r, b).start()
            return carry
        lax.fori_loop(0, tm, body, 0, unroll=DMA_ISSUE_UNROLL)

    def start_scatter(n_rows):
        def body(r, carry):
            scatter_row(tok_ref[0, r], r).start()
            return carry

        @pl.when(n_rows == tm)
        def _():
            lax.fori_loop(0, tm, body, 0, unroll=DMA_ISSUE_UNROLL)

        @pl.when(n_rows != tm)
        def _():
            lax.fori_loop(0, n_rows, body, 0)

    def wait_scatter(n_rows):
        def body(r, carry):
            scatter_row(0, r).wait()
            return carry

        @pl.when(n_rows == tm)
        def _():
            pltpu.make_async_copy(ybuf, ybuf, scatter_sem).wait()

        @pl.when(n_rows != tm)
        def _():
            lax.fori_loop(0, n_rows, body, 0)

    @pl.when(i == 0)
    def _():
        start_gather(tok_ref, 0)

    @pl.when(i < n_tiles)
    def _():
        pltpu.make_async_copy(xbuf.at[buf], xbuf.at[buf], gather_sem.at[buf]).wait()

        @pl.when(i + 1 < n_tiles)
        def _():
            start_gather(tok_next_ref, 1 - buf)

        group = tile_group_ref[i]
        x = xbuf[buf]
        logits = _router_logits(x, wr_ref, br_ref)
        lane = lax.broadcasted_iota(jnp.int32, logits.shape, 1)
        neg = -jnp.inf
        g_log = jnp.where(_group_lanes(lane), logits, neg)
        g_max = jnp.max(g_log, axis=-1, keepdims=True)
        p_group = 1.0 / jnp.sum(jnp.exp(g_log - g_max), axis=-1, keepdims=True)
        first = group * EXPERTS_PER_GROUP
        e_log = jnp.where((lane >= first) & (lane < first + EXPERTS_PER_GROUP), logits, neg)
        v1 = jnp.max(e_log, axis=-1, keepdims=True)
        i1 = jnp.min(jnp.where(e_log == v1, lane, ROUTER_NO_LANE), axis=-1, keepdims=True)
        e_log2 = jnp.where(lane == i1, neg, e_log)
        v2 = jnp.max(e_log2, axis=-1, keepdims=True)
        i2 = jnp.min(jnp.where(e_log2 == v2, lane, ROUTER_NO_LANE), axis=-1, keepdims=True)
        e2 = jnp.exp(v2 - v1)
        w1 = p_group / (1.0 + e2)
        w2 = p_group * e2 / (1.0 + e2)
        combine = jnp.where(lane == i1, w1, 0.0) + jnp.where(lane == i2, w2, 0.0)

        xb = x.astype(BF16)
        hidden = []
        for e in range(EXPERTS_PER_GROUP):
            w_tok = jnp.sum(jnp.where(lane == first + e, combine, 0.0), axis=-1, keepdims=True)
            hg = _dot(xb, wg_ref[e])
            hu = _dot(xb, wu_ref[e])
            hidden.append((hg * jax.nn.sigmoid(hg) * hu * w_tok).astype(BF16))
        y = _dot(jnp.concatenate(hidden, axis=1), wd_ref[...])

        @pl.when(i >= 1)
        def _():
            wait_scatter(tile_valid_ref[jnp.maximum(i - 1, 0)])

        ybuf[...] = y
        n_valid = tile_valid_ref[i]
        start_scatter(n_valid)

        @pl.when(i == n_tiles - 1)
        def _():
            wait_scatter(n_valid)


def _experts(u2, gidx, w_r, b_r, w_g, w_u, w_d):
    n, d = u2.shape
    tm = MOE_ROWS
    d_e = w_g.shape[-1]
    tile_group, tile_valid, n_tiles, slot_token = _route_plan(gidx, tm)
    max_tiles = slot_token.shape[0]
    w_g = w_g.reshape(N_GROUPS, EXPERTS_PER_GROUP, d, d_e)
    w_u = w_u.reshape(N_GROUPS, EXPERTS_PER_GROUP, d, d_e)
    w_d = w_d.reshape(N_GROUPS, EXPERTS_PER_GROUP * d_e, d)
    grid_spec = pltpu.PrefetchScalarGridSpec(
        num_scalar_prefetch=3,
        grid=(max_tiles,),
        in_specs=[pl.BlockSpec((None, 1, tm), lambda i, tg, tv, nt: (i, 0, 0), memory_space=pltpu.SMEM),
                  pl.BlockSpec((None, 1, tm), lambda i, tg, tv, nt: (jnp.minimum(i + 1, max_tiles - 1), 0, 0),
                               memory_space=pltpu.SMEM),
                  pl.BlockSpec(memory_space=pl.ANY),
                  pl.BlockSpec(w_r.shape, lambda i, tg, tv, nt: (0, 0)),
                  pl.BlockSpec(b_r.shape, lambda i, tg, tv, nt: (0, 0)),
                  pl.BlockSpec((None, EXPERTS_PER_GROUP, d, d_e), lambda i, tg, tv, nt: (tg[i], 0, 0, 0)),
                  pl.BlockSpec((None, EXPERTS_PER_GROUP, d, d_e), lambda i, tg, tv, nt: (tg[i], 0, 0, 0)),
                  pl.BlockSpec((None, EXPERTS_PER_GROUP * d_e, d), lambda i, tg, tv, nt: (tg[i], 0, 0))],
        out_specs=pl.BlockSpec(memory_space=pl.ANY),
        scratch_shapes=[pltpu.VMEM((2, tm, d), F32),
                        pltpu.VMEM((tm, d), F32),
                        pltpu.SemaphoreType.DMA((2,)),
                        pltpu.SemaphoreType.DMA(())],
    )
    return pl.pallas_call(
        functools.partial(_experts_kernel, tm=tm),
        grid_spec=grid_spec,
        out_shape=jax.ShapeDtypeStruct((n, d), F32),
        compiler_params=_cparams(("arbitrary",)),
        name="experts",
    )(tile_group, tile_valid, n_tiles, slot_token, slot_token, u2, w_r, b_r, w_g, w_u, w_d)


def _ln2_kernel(x1_ref, y_ref, gate2_ref, ln_g_ref, ln_b_ref, o_ref):
    v = DEEPNORM_ALPHA * x1_ref[...] + gate2_ref[...] * y_ref[...]
    o_ref[...] = _layer_norm(v, ln_g_ref[...], ln_b_ref[...])


def _ln2(x1, y, mod4, ln_g, ln_b, seq):
    n, d = x1.shape
    tm = min(LN2_ROWS, seq)
    tiles_per_seq = seq // tm
    rows = pl.BlockSpec((tm, d), lambda i: (i, 0))
    vec = pl.BlockSpec((1, d), lambda i: (0, 0))
    return pl.pallas_call(
        _ln2_kernel,
        grid=(n // tm,),
        in_specs=[rows, rows, pl.BlockSpec((None, None, 1, d), lambda i: (i // tiles_per_seq, 5, 0, 0)), vec, vec],
        out_specs=rows,
        out_shape=jax.ShapeDtypeStruct((n, d), F32),
        compiler_params=_cparams(("arbitrary",)),
        name="ln2",
    )(x1, y, mod4, ln_g, ln_b)


def kernel(x, c, positions, w_ada, b_ada, w_in, mla_q_norm_g, w_q_up, mla_kv_norm_g, w_kv_up, w_branch_sb, w_branch_mla, w_out, ln1_g, ln1_b, w_router_group, b_router_group, w_router_expert, b_router_expert, w_exp_gate, w_exp_up, w_exp_down, ln2_g, ln2_b):
    bsz, seq, d = x.shape
    n = bsz * seq
    invf = _rope_inv_freq_tile()
    posf = positions.astype(F32).reshape(n, 1)
    tri = _suffix_sum_matrix()
    for l in range(w_ada.shape[0]):
        x2 = x.reshape(n, d)
        mod4 = _adaln_mod(c, w_ada[l], b_ada[l]).reshape(bsz, N_MOD, 1, d)
        weights = _prep_in_weights(w_in[l], w_q_up[l], w_kv_up[l])
        sb, qm, km, vm, gates = _in_proj(x2, mod4, posf, invf, weights,
                                         mla_q_norm_g[l].reshape(1, -1), mla_kv_norm_g[l].reshape(1, -1), seq)
        o_sb = _sb_attention(sb.reshape(bsz, seq, -1), tri).reshape(n, SB_WIDTH)
        o_mla = _mla_attention(qm.reshape(bsz, seq, -1), km.reshape(bsz, seq, -1),
                               vm.reshape(bsz, seq, -1)).reshape(n, -1)
        w_r = jnp.zeros((d, LANES), F32)
        w_r = w_r.at[:, :N_EXPERTS].set(w_router_expert[l]).at[:, N_EXPERTS:N_EXPERTS + N_GROUPS].set(w_router_group[l])
        b_r = jnp.zeros((1, LANES), F32)
        b_r = b_r.at[0, :N_EXPERTS].set(b_router_expert[l]).at[0, N_EXPERTS:N_EXPERTS + N_GROUPS].set(b_router_group[l])
        w_r = w_r.astype(BF16)
        x1, u2, gidx = _merge(o_sb, o_mla, gates, x2, mod4,
                              w_branch_sb[l].astype(BF16), w_branch_mla[l].astype(BF16), w_out[l].astype(BF16),
                              ln1_g[l].reshape(1, d), ln1_b[l].reshape(1, d), w_r, b_r, seq)
        y = _experts(u2, gidx.reshape(n), w_r, b_r, w_exp_gate[l].astype(BF16), w_exp_up[l].astype(BF16),
                     w_exp_down[l].astype(BF16))
        out = _ln2(x1, y, mod4, ln2_g[l].reshape(1, d), ln2_b[l].reshape(1, d), seq)
        x = out.reshape(bsz, seq, d)
    return x
```

```python
import functools
import math

import jax
import jax.numpy as jnp
import numpy as np
from jax import lax
from jax.experimental import pallas as pl
from jax.experimental.pallas import tpu as pltpu

F32 = jnp.float32
BF16 = jnp.bfloat16

D_MODEL = 1024
SB_HEADS = 8
SB_HEAD_DIM = 64
SB_WIDTH = SB_HEADS * SB_HEAD_DIM
MLA_HEADS = 8
MLA_NOPE_DIM = 64
MLA_ROPE_DIM = 32
MLA_V_DIM = 64
MLA_Q_RANK = 384
MLA_KV_RANK = 256
MLA_QK_DIM = MLA_NOPE_DIM + MLA_ROPE_DIM
MLA_WIDTH = MLA_HEADS * MLA_V_DIM
ROPE_THETA = 10000.0
N_GROUPS = 4
EXPERTS_PER_GROUP = 8
N_EXPERTS = N_GROUPS * EXPERTS_PER_GROUP
D_EXPERT = 256
DEPTH = 1
DEEPNORM_ALPHA = (2.0 * DEPTH) ** 0.25
LN_EPS = 1e-5
RMS_EPS = 1e-6
LOG2_E = math.log2(math.e)
MASK_BIAS = -1e30
N_MOD = 6

LANES = 128
MLA_HEAD_LANES = 128
VMEM_LIMIT_BYTES = 56 * 1024 * 1024

PROJ_ROWS = 256
ATTN_Q = 256
ATTN_K = 256
MERGE_ROWS = 256
MOE_ROWS = 512
LN2_ROWS = 512
DMA_ISSUE_UNROLL = 8
SB_PAIRS_PER_STEP = 2
MLA_HEADS_PER_STEP = 4


def _cparams(sem):
    return pltpu.CompilerParams(dimension_semantics=sem, vmem_limit_bytes=VMEM_LIMIT_BYTES)


def _dot(a, b):
    return jnp.dot(a, b, preferred_element_type=F32)


def _dot_t(a, b):
    return lax.dot_general(a, b, (((1,), (1,)), ((), ())), preferred_element_type=F32)


def _adaln_kernel(c_ref, w_ref, b_ref, o_ref):
    c = c_ref[...]
    ca = c * jax.nn.sigmoid(c)
    o_ref[...] = _dot(ca.astype(BF16), w_ref[...].astype(BF16)) + b_ref[...]


def _adaln_mod(c, w_ada, b_ada):
    bsz, d = c.shape
    return pl.pallas_call(
        _adaln_kernel,
        grid=(N_MOD,),
        in_specs=[pl.BlockSpec((bsz, d), lambda j: (0, 0)),
                  pl.BlockSpec((d, d), lambda j: (0, j)),
                  pl.BlockSpec((1, d), lambda j: (0, j))],
        out_specs=pl.BlockSpec((bsz, d), lambda j: (0, j)),
        out_shape=jax.ShapeDtypeStruct((bsz, N_MOD * d), F32),
        compiler_params=_cparams(("arbitrary",)),
        name="adaln_mod",
    )(c, w_ada, b_ada.reshape(1, N_MOD * d))


def _in_proj_kernel(x_ref, scale_ref, shift_ref, pos_ref, invf_ref,
                    w_sb_ref, w_dn_ref, w_kr_ref, w_g_ref, qg_ref, kvg_ref,
                    w_qu_ref, w_ku_ref, w_vu_ref,
                    sb_ref, qm_ref, km_ref, vm_ref, g_ref):
    u = (x_ref[...] * (1.0 + scale_ref[...]) + shift_ref[...]).astype(BF16)

    sb = _dot(u, w_sb_ref[...])
    sb_scale = LOG2_E / math.sqrt(SB_HEAD_DIM)
    sb_ref[:, :SB_WIDTH] = (sb[:, :SB_WIDTH] * sb_scale).astype(BF16)
    sb_ref[:, SB_WIDTH:] = sb[:, SB_WIDTH:].astype(BF16)

    g_ref[...] = _dot(u, w_g_ref[...]).astype(BF16)

    ang = pos_ref[...] * invf_ref[...]
    cos = jnp.cos(ang)
    sin = jnp.sin(ang)

    dn = _dot(u, w_dn_ref[...])
    q_dn = dn[:, :MLA_Q_RANK]
    kv_dn = dn[:, MLA_Q_RANK:]
    qn = q_dn * lax.rsqrt(jnp.mean(q_dn * q_dn, axis=-1, keepdims=True) + RMS_EPS) * qg_ref[...]
    kvn = kv_dn * lax.rsqrt(jnp.mean(kv_dn * kv_dn, axis=-1, keepdims=True) + RMS_EPS) * kvg_ref[...]
    qn = qn.astype(BF16)
    kvn = kvn.astype(BF16)

    width = MLA_HEADS * MLA_HEAD_LANES
    qu = _dot(qn, w_qu_ref[...])
    ku = _dot(kvn, w_ku_ref[...])
    vu = _dot(kvn, w_vu_ref[...])
    v_lane = lax.broadcasted_iota(jnp.int32, vu.shape, 1) % MLA_HEAD_LANES
    vm_ref[...] = jnp.where(v_lane < MLA_V_DIM, vu, 1.0).astype(BF16)
    kr = _dot(u, w_kr_ref[...])
    kr = kr[:, :LANES] * cos + kr[:, LANES:] * sin
    mla_scale = LOG2_E / math.sqrt(MLA_QK_DIM)
    for h in range(MLA_HEADS):
        sl = slice(h * MLA_HEAD_LANES, (h + 1) * MLA_HEAD_LANES)
        sl_rot = slice(width + h * MLA_HEAD_LANES, width + (h + 1) * MLA_HEAD_LANES)
        qm_ref[:, sl] = ((qu[:, sl] * cos + qu[:, sl_rot] * sin) * mla_scale).astype(BF16)
        km_ref[:, sl] = (ku[:, sl] + kr).astype(BF16)


def _rope_inv_freq_tile():
    inv_freq = 1.0 / (ROPE_THETA ** (jnp.arange(0, MLA_ROPE_DIM, 2, dtype=F32) / MLA_ROPE_DIM))
    tile = jnp.zeros((LANES,), F32)
    half = MLA_ROPE_DIM // 2
    tile = tile.at[MLA_NOPE_DIM:MLA_NOPE_DIM + half].set(inv_freq)
    tile = tile.at[MLA_NOPE_DIM + half:MLA_NOPE_DIM + 2 * half].set(inv_freq)
    return tile.reshape(1, LANES)


def _rotate_half_cols(w):
    half = MLA_ROPE_DIM // 2
    return jnp.concatenate([-w[..., half:], w[..., :half]], axis=-1)


def _prep_in_weights(w_in, w_q_up, w_kv_up):
    o = 0
    w_sb = w_in[:, o:o + 3 * SB_WIDTH]; o += 3 * SB_WIDTH
    w_dn = w_in[:, o:o + MLA_Q_RANK + MLA_KV_RANK]; o += MLA_Q_RANK + MLA_KV_RANK
    w_kr = w_in[:, o:o + MLA_ROPE_DIM]; o += MLA_ROPE_DIM
    w_g = w_in[:, o:]

    def rope_tile(w):
        z = jnp.zeros((w.shape[0], LANES), w.dtype)
        return z.at[:, MLA_NOPE_DIM:MLA_NOPE_DIM + MLA_ROPE_DIM].set(w)

    w_kr2 = jnp.concatenate([rope_tile(w_kr), rope_tile(_rotate_half_cols(w_kr))], axis=1)

    wq = w_q_up.reshape(MLA_Q_RANK, MLA_HEADS, MLA_QK_DIM)
    q_base = jnp.zeros((MLA_Q_RANK, MLA_HEADS, MLA_HEAD_LANES), w_q_up.dtype).at[:, :, :MLA_QK_DIM].set(wq)
    q_rot = jnp.zeros((MLA_Q_RANK, MLA_HEADS, MLA_HEAD_LANES), w_q_up.dtype)
    q_rot = q_rot.at[:, :, MLA_NOPE_DIM:MLA_QK_DIM].set(_rotate_half_cols(wq[:, :, MLA_NOPE_DIM:]))
    w_qu = jnp.concatenate([q_base.reshape(MLA_Q_RANK, -1), q_rot.reshape(MLA_Q_RANK, -1)], axis=1)

    wkv = w_kv_up.reshape(MLA_KV_RANK, MLA_HEADS, MLA_NOPE_DIM + MLA_V_DIM)
    w_ku = jnp.zeros((MLA_KV_RANK, MLA_HEADS, MLA_HEAD_LANES), w_kv_up.dtype)
    w_ku = w_ku.at[:, :, :MLA_NOPE_DIM].set(wkv[:, :, :MLA_NOPE_DIM]).reshape(MLA_KV_RANK, -1)
    w_vu = jnp.zeros((MLA_KV_RANK, MLA_HEADS, MLA_HEAD_LANES), w_kv_up.dtype)
    w_vu = w_vu.at[:, :, :MLA_V_DIM].set(wkv[:, :, MLA_NOPE_DIM:]).reshape(MLA_KV_RANK, -1)
    return tuple(w.astype(BF16) for w in (w_sb, w_dn, w_kr2, w_g, w_qu, w_ku, w_vu))


def _in_proj(x2, mod4, posf, invf, weights, qg, kvg, seq):
    n, d = x2.shape
    tm = min(PROJ_ROWS, seq)
    tiles_per_seq = seq // tm
    w_sb, w_dn, w_kr2, w_g, w_qu, w_ku, w_vu = weights

    def whole(a):
        return pl.BlockSpec(a.shape, lambda i: (0,) * a.ndim)

    def mod_spec(k):
        return pl.BlockSpec((None, None, 1, d), lambda i: (i // tiles_per_seq, k, 0, 0))

    def rows(width):
        return pl.BlockSpec((tm, width), lambda i: (i, 0))

    width = MLA_HEADS * MLA_HEAD_LANES
    return pl.pallas_call(
        _in_proj_kernel,
        grid=(n // tm,),
        in_specs=[rows(d), mod_spec(1), mod_spec(0), rows(1), whole(invf),
                  whole(w_sb), whole(w_dn), whole(w_kr2), whole(w_g), whole(qg), whole(kvg),
                  whole(w_qu), whole(w_ku), whole(w_vu)],
        out_specs=[rows(3 * SB_WIDTH), rows(width), rows(width), rows(width), rows(2 * d)],
        out_shape=[jax.ShapeDtypeStruct((n, 3 * SB_WIDTH), BF16),
                   jax.ShapeDtypeStruct((n, width), BF16),
                   jax.ShapeDtypeStruct((n, width), BF16),
                   jax.ShapeDtypeStruct((n, width), BF16),
                   jax.ShapeDtypeStruct((n, 2 * d), BF16)],
        compiler_params=_cparams(("arbitrary",)),
        name="in_proj",
    )(x2, mod4, mod4, posf, invf, w_sb, w_dn, w_kr2, w_g, qg, kvg, w_qu, w_ku, w_vu)


def _item_schedule(n_q):
    q_of, t_of = [], []
    for qi in range(n_q):
        for t in range(qi + 1):
            q_of.append(qi)
            t_of.append(t)
    return np.asarray(q_of, np.int32), np.asarray(t_of, np.int32)


def _causal_bias(tq, tk, strict):
    r = jnp.arange(tq)[:, None]
    c = jnp.arange(tk)[None, :]
    visible = (c < r) if strict else (c <= r)
    return jnp.stack([jnp.zeros((tq, tk), F32), jnp.where(visible, 0.0, MASK_BIAS).astype(F32)])


def _pipeline3(n, stage_a, stage_b, stage_c, newest_first):
    stage_a(0, 0)
    if n >= 2:
        stage_a(1, 1)
    stage_b(0, 0)

    def step(i, slot):
        stages = [lambda: stage_a(i, slot), lambda: stage_b(i - 1, 1 - slot), lambda: stage_c(i - 2, slot)]
        for run in (stages if newest_first else reversed(stages)):
            run()

    def body(k, carry):
        step(2 + 2 * k, 0)
        step(3 + 2 * k, 1)
        return carry

    pairs = max(n - 2, 0) // 2
    lax.fori_loop(0, pairs, body, 0)
    if n >= 3 and n % 2 == 1:
        step(n - 1, 0)
    if n >= 2:
        stage_c(n - 2, n % 2)
        stage_b(n - 1, (n - 1) % 2)
    stage_c(n - 1, (n - 1) % 2)


def _suffix_sum_matrix():
    j = jnp.arange(2 * LANES)[:, None] % LANES
    n = jnp.arange(2 * LANES)[None, :]
    return jnp.where(n < LANES, (j >= n), True).astype(F32)


def _sb_kernel(item_q_ref, item_t_ref, q_ref, k_ref, v_ref, tri_ref, bias_ref, o_ref,
               qh_ref, c_ref, acc_ref, z_ref, w_ref, *, tq, tk, n_pairs, n_items):
    lane = lax.broadcasted_iota(jnp.int32, (tq, LANES), 1)
    n_sub = tk // LANES
    n_chains = 2 * n_pairs

    for p in range(n_pairs):
        qp = q_ref[:, p * LANES:(p + 1) * LANES]
        q_lane = lax.broadcasted_iota(jnp.int32, qp.shape, 1)
        for h in range(2):
            head_lanes = (q_lane < SB_HEAD_DIM) if h == 0 else (q_lane >= SB_HEAD_DIM)
            qh_ref[2 * p + h] = jnp.where(head_lanes, qp, jnp.zeros_like(qp))
    c_ref[...] = jnp.zeros_like(c_ref)
    acc_ref[...] = jnp.zeros_like(acc_ref)

    def item(f):
        qi = item_q_ref[f]
        t = item_t_ref[f]
        q_rows = pl.ds(pl.multiple_of(qi * tq, tq), tq)
        k_rows = pl.ds(pl.multiple_of((qi - t) * tk, tk), tk)
        return q_rows, k_rows, t

    def pair_lanes(ci):
        return slice((ci // 2) * LANES, (ci // 2 + 1) * LANES)

    def scores(f, slot):
        q_rows, k_rows, t = item(f)
        bias = bias_ref[(t == 0).astype(jnp.int32)]
        for ci in range(n_chains):
            z_ref[slot, ci] = _dot_t(qh_ref[ci, q_rows, :], k_ref[k_rows, pair_lanes(ci)]) + bias

    def weights(f, slot):
        _, _, t = item(f)
        keep = jnp.where(t == 0, 0.0, 1.0)
        tri = tri_ref[...]
        sign = jnp.uint32(0x80000000)
        bf16_bits = jnp.uint32(0xFFFF0000)
        for ci in range(n_chains):
            c = c_ref[ci] * keep
            for s in reversed(range(n_sub)):
                z = z_ref[slot, ci, :, s * LANES:(s + 1) * LANES]
                neg_abs = pltpu.bitcast(pltpu.bitcast(z, jnp.uint32) | sign, F32)
                sp = jnp.maximum(z, 0.0) + jnp.log2(1.0 + jnp.exp2(neg_abs))
                hi = pltpu.bitcast(pltpu.bitcast(sp, jnp.uint32) & bf16_bits, F32)
                r = _dot(jnp.concatenate([hi, sp - hi], axis=1), tri)
                w_ref[slot, ci, :, s * LANES:(s + 1) * LANES] = jnp.exp2(z - (c + r[:, :LANES])).astype(BF16)
                c = c + r[:, LANES:]
            c_ref[ci] = c

    def values(f, slot):
        q_rows, k_rows, t = item(f)
        keep = jnp.where(t == 0, 0.0, 1.0)
        for p in range(n_pairs):
            accs = []
            for ci in (2 * p, 2 * p + 1):
                acc = acc_ref[ci] * keep + _dot(w_ref[slot, ci], v_ref[k_rows, pair_lanes(ci)])
                acc_ref[ci] = acc
                accs.append(acc)
            o_ref[q_rows, p * LANES:(p + 1) * LANES] = jnp.where(lane < SB_HEAD_DIM, accs[0], accs[1]).astype(o_ref.dtype)

    _pipeline3(n_items, scores, weights, values, newest_first=True)


def _sb_attention(sb3, tri):
    bsz, seq, _ = sb3.shape
    tq = min(ATTN_Q, seq)
    tk = tq
    n_pairs = SB_PAIRS_PER_STEP
    width = n_pairs * LANES
    groups = SB_WIDTH // width
    item_q, item_t = _item_schedule(seq // tq)
    bias = _causal_bias(tq, tk, strict=True)

    def seq_block(first_block):
        return pl.BlockSpec((None, seq, width), lambda b, g, iq, it: (b, 0, first_block + g))

    grid_spec = pltpu.PrefetchScalarGridSpec(
        num_scalar_prefetch=2,
        grid=(bsz, groups),
        in_specs=[seq_block(0), seq_block(groups), seq_block(2 * groups),
                  pl.BlockSpec(tri.shape, lambda b, g, iq, it: (0, 0)),
                  pl.BlockSpec(bias.shape, lambda b, g, iq, it: (0, 0, 0))],
        out_specs=seq_block(0),
        scratch_shapes=[pltpu.VMEM((2 * n_pairs, seq, LANES), BF16),
                        pltpu.VMEM((2 * n_pairs, tq, LANES), F32),
                        pltpu.VMEM((2 * n_pairs, tq, LANES), F32),
                        pltpu.VMEM((2, 2 * n_pairs, tq, tk), F32),
                        pltpu.VMEM((2, 2 * n_pairs, tq, tk), BF16)],
    )
    return pl.pallas_call(
        functools.partial(_sb_kernel, tq=tq, tk=tk, n_pairs=n_pairs, n_items=len(item_q)),
        grid_spec=grid_spec,
        out_shape=jax.ShapeDtypeStruct((bsz, seq, SB_WIDTH), BF16),
        compiler_params=_cparams(("arbitrary", "arbitrary")),
        name="sb_attn",
    )(item_q, item_t, sb3, sb3, sb3, tri, bias)


def _mla_kernel(item_q_ref, item_t_ref, q_ref, k_ref, v_ref, bias_ref, o_ref,
                m_ref, s_ref, p_ref, alpha_ref, *, tq, tk, n_heads, n_items):
    n_sub = tk // LANES
    m_ref[...] = jnp.zeros_like(m_ref)
    o_ref[...] = jnp.zeros_like(o_ref)

    def item(f):
        qi = item_q_ref[f]
        t = item_t_ref[f]
        q_rows = pl.ds(pl.multiple_of(qi * tq, tq), tq)
        k_rows = pl.ds(pl.multiple_of((qi - t) * tk, tk), tk)
        return q_rows, k_rows, t

    def head_lanes(ci):
        return slice(ci * MLA_HEAD_LANES, (ci + 1) * MLA_HEAD_LANES)

    def scores(f, slot):
        q_rows, k_rows, t = item(f)
        bias = bias_ref[(t == 0).astype(jnp.int32)]
        for ci in range(n_heads):
            s_ref[slot, ci] = _dot_t(q_ref[q_rows, head_lanes(ci)], k_ref[k_rows, head_lanes(ci)]) + bias

    def numerators(f, slot):
        _, _, t = item(f)
        for ci in range(n_heads):
            row_max = jnp.max(s_ref[slot, ci], axis=-1, keepdims=True)
            m = jnp.where(t == 0, -jnp.inf, m_ref[ci])
            m_new = jnp.maximum(m, jnp.broadcast_to(row_max, (tq, LANES)))
            alpha_ref[slot, ci] = jnp.exp2(m - m_new)
            for j in range(n_sub):
                sl = slice(j * LANES, (j + 1) * LANES)
                p_ref[slot, ci, :, sl] = jnp.exp2(s_ref[slot, ci, :, sl] - m_new).astype(BF16)
            m_ref[ci] = m_new

    def values(f, slot):
        q_rows, k_rows, _ = item(f)
        for ci in range(n_heads):
            acc = o_ref[q_rows, head_lanes(ci)]
            o_ref[q_rows, head_lanes(ci)] = (alpha_ref[slot, ci] * acc
                                             + _dot(p_ref[slot, ci], v_ref[k_rows, head_lanes(ci)]))

    _pipeline3(n_items, scores, numerators, values, newest_first=False)


def _mla_attention(qm3, km3, vm3):
    bsz, seq, _ = qm3.shape
    tq = min(ATTN_Q, seq)
    tk = tq
    n_heads = MLA_HEADS_PER_STEP
    groups = MLA_HEADS // n_heads
    width = n_heads * MLA_HEAD_LANES
    item_q, item_t = _item_schedule(seq // tq)
    bias = _causal_bias(tq, tk, strict=False)
    seq_block = pl.BlockSpec((None, seq, width), lambda b, g, iq, it: (b, 0, g))
    grid_spec = pltpu.PrefetchScalarGridSpec(
        num_scalar_prefetch=2,
        grid=(bsz, groups),
        in_specs=[seq_block, seq_block, seq_block, pl.BlockSpec(bias.shape, lambda b, g, iq, it: (0, 0, 0))],
        out_specs=seq_block,
        scratch_shapes=[pltpu.VMEM((n_heads, tq, LANES), F32),
                        pltpu.VMEM((2, n_heads, tq, tk), F32),
                        pltpu.VMEM((2, n_heads, tq, tk), BF16),
                        pltpu.VMEM((2, n_heads, tq, LANES), F32)],
    )
    return pl.pallas_call(
        functools.partial(_mla_kernel, tq=tq, tk=tk, n_heads=n_heads, n_items=len(item_q)),
        grid_spec=grid_spec,
        out_shape=jax.ShapeDtypeStruct((bsz, seq, MLA_HEADS * MLA_HEAD_LANES), F32),
        compiler_params=_cparams(("arbitrary", "arbitrary")),
        name="mla_attn",
    )(item_q, item_t, qm3, km3, vm3, bias)


def _layer_norm(v, g, b):
    mu = jnp.mean(v, axis=-1, keepdims=True)
    dv = v - mu
    var = jnp.mean(dv * dv, axis=-1, keepdims=True)
    return dv * lax.rsqrt(var + LN_EPS) * g + b


ROUTER_NO_LANE = 1 << 20


def _router_logits(u2, wr_ref, br_ref):
    return _dot(u2.astype(BF16), wr_ref[...]) + br_ref[...]


def _group_lanes(lane):
    return (lane >= N_EXPERTS) & (lane < N_EXPERTS + N_GROUPS)


def _merge_kernel(osb_ref, omla_ref, g_ref, x_ref, gate1_ref, scale2_ref, shift2_ref,
                  wsb_ref, wmla_ref, wout_ref, ln_g_ref, ln_b_ref, wr_ref, br_ref,
                  x1_ref, u2_ref, gidx_ref):
    d = x_ref.shape[-1]
    y_sb = _dot(osb_ref[...], wsb_ref[...])
    lane = lax.broadcasted_iota(jnp.int32, (osb_ref.shape[0], LANES), 1)
    o_mla = []
    for p in range(MLA_HEADS // 2):
        res = []
        for h in range(2):
            acc = omla_ref[:, (2 * p + h) * MLA_HEAD_LANES:(2 * p + h + 1) * MLA_HEAD_LANES]
            res.append(acc / pltpu.roll(acc, MLA_V_DIM, 1))
        o_mla.append(jnp.where(lane < MLA_V_DIM, res[0], pltpu.roll(res[1], MLA_V_DIM, 1)).astype(BF16))
    y_mla = _dot(jnp.concatenate(o_mla, axis=1), wmla_ref[...])
    gates = g_ref[...].astype(F32)
    mixed = jax.nn.sigmoid(gates[:, :d]) * y_sb + jax.nn.sigmoid(gates[:, d:]) * y_mla
    attn = _dot(mixed.astype(BF16), wout_ref[...])
    x1 = _layer_norm(DEEPNORM_ALPHA * x_ref[...] + gate1_ref[...] * attn, ln_g_ref[...], ln_b_ref[...])
    x1_ref[...] = x1
    u2 = x1 * (1.0 + scale2_ref[...]) + shift2_ref[...]
    u2_ref[...] = u2
    logits = _router_logits(u2, wr_ref, br_ref)
    lane = lax.broadcasted_iota(jnp.int32, logits.shape, 1)
    g_log = jnp.where(_group_lanes(lane), logits, -jnp.inf)
    g_max = jnp.max(g_log, axis=-1, keepdims=True)
    gidx_ref[...] = jnp.min(jnp.where(g_log == g_max, lane, ROUTER_NO_LANE), axis=-1, keepdims=True) - N_EXPERTS


def _merge(o_sb, o_mla, gates, x2, mod4, w_bsb, w_bmla, w_out, ln_g, ln_b, w_r, b_r, seq):
    n, d = x2.shape
    tm = min(MERGE_ROWS, seq)
    tiles_per_seq = seq // tm

    def whole(a):
        return pl.BlockSpec(a.shape, lambda i: (0,) * a.ndim)

    def mod_spec(k):
        return pl.BlockSpec((None, None, 1, d), lambda i: (i // tiles_per_seq, k, 0, 0))

    def rows(width):
        return pl.BlockSpec((tm, width), lambda i: (i, 0))

    return pl.pallas_call(
        _merge_kernel,
        grid=(n // tm,),
        in_specs=[rows(SB_WIDTH), rows(MLA_HEADS * MLA_HEAD_LANES), rows(2 * d), rows(d),
                  mod_spec(2), mod_spec(4), mod_spec(3),
                  whole(w_bsb), whole(w_bmla), whole(w_out), whole(ln_g), whole(ln_b), whole(w_r), whole(b_r)],
        out_specs=[rows(d), rows(d), rows(1)],
        out_shape=[jax.ShapeDtypeStruct((n, d), F32),
                   jax.ShapeDtypeStruct((n, d), F32),
                   jax.ShapeDtypeStruct((n, 1), jnp.int32)],
        compiler_params=_cparams(("arbitrary",)),
        name="merge",
    )(o_sb, o_mla, gates, x2, mod4, mod4, mod4, w_bsb, w_bmla, w_out, ln_g, ln_b, w_r, b_r)


def _route_plan(gidx, tm):
    n = gidx.shape[0]
    max_tiles = n // tm + N_GROUPS
    order = jnp.argsort(gidx, stable=True).astype(jnp.int32)
    counts = jnp.sum(gidx[:, None] == jnp.arange(N_GROUPS, dtype=jnp.int32)[None, :], axis=0).astype(jnp.int32)
    tiles = (counts + tm - 1) // tm
    tile_end = jnp.cumsum(tiles)
    tile_start = tile_end - tiles
    run_start = jnp.cumsum(counts) - counts
    tile_ids = jnp.arange(max_tiles, dtype=jnp.int32)
    n_tiles = tile_end[-1]
    tile_group = jnp.minimum(jnp.sum(tile_ids[:, None] >= tile_end[None, :], axis=1), N_GROUPS - 1).astype(jnp.int32)
    rows_before = (tile_ids - tile_start[tile_group]) * tm
    tile_valid = jnp.clip(counts[tile_group] - rows_before, 0, tm)
    tile_valid = jnp.where(tile_ids < n_tiles, tile_valid, 0).astype(jnp.int32)
    row = jnp.arange(tm, dtype=jnp.int32)[None, :]
    sorted_pos = (run_start[tile_group] + rows_before)[:, None] + row
    slot_token = jnp.where(row < tile_valid[:, None], order[jnp.clip(sorted_pos, 0, n - 1)], 0)
    return tile_group, tile_valid, n_tiles.reshape(1).astype(jnp.int32), slot_token.reshape(max_tiles, 1, tm)


def _experts_kernel(tile_group_ref, tile_valid_ref, n_tiles_ref, tok_ref, tok_next_ref,
                    u2_hbm, wr_ref, br_ref, wg_ref, wu_ref, wd_ref, y_hbm,
                    xbuf, ybuf, gather_sem, scatter_sem, *, tm):
    i = pl.program_id(0)
    n_tiles = n_tiles_ref[0]
    buf = i % 2

    def gather_row(tok, r, b):
        return pltpu.make_async_copy(u2_hbm.at[pl.ds(tok, 1)], xbuf.at[b, pl.ds(r, 1)], gather_sem.at[b])

    def scatter_row(tok, r):
        return pltpu.make_async_copy(ybuf.at[pl.ds(r, 1)], y_hbm.at[pl.ds(tok, 1)], scatter_sem)

    def start_gather(idx_ref, b):
        def body(r, carry):
            gather_row(idx_ref[0, r], r, b).start()
            return carry
        lax.fori_loop(0, tm, body, 0, unroll=DMA_ISSUE_UNROLL)

    def start_scatter(n_rows):
        def body(r, carry):
            scatter_row(tok_ref[0, r], r).start()
            return carry

        @pl.when(n_rows == tm)
        def _():
            lax.fori_loop(0, tm, body, 0, unroll=DMA_ISSUE_UNROLL)

        @pl.when(n_rows != tm)
        def _():
            lax.fori_loop(0, n_rows, body, 0)

    def wait_scatter(n_rows):
        def body(r, carry):
            scatter_row(0, r).wait()
            return carry

        @pl.when(n_rows == tm)
        def _():
            pltpu.make_async_copy(ybuf, ybuf, scatter_sem).wait()

        @pl.when(n_rows != tm)
        def _():
            lax.fori_loop(0, n_rows, body, 0)

    @pl.when(i == 0)
    def _():
        start_gather(tok_ref, 0)

    @pl.when(i < n_tiles)
    def _():
        pltpu.make_async_copy(xbuf.at[buf], xbuf.at[buf], gather_sem.at[buf]).wait()

        @pl.when(i + 1 < n_tiles)
        def _():
            start_gather(tok_next_ref, 1 - buf)

        group = tile_group_ref[i]
        x = xbuf[buf]
        logits = _router_logits(x, wr_ref, br_ref)
        lane = lax.broadcasted_iota(jnp.int32, logits.shape, 1)
        neg = -jnp.inf
        g_log = jnp.where(_group_lanes(lane), logits, neg)
        g_max = jnp.max(g_log, axis=-1, keepdims=True)
        p_group = 1.0 / jnp.sum(jnp.exp(g_log - g_max), axis=-1, keepdims=True)
        first = group * EXPERTS_PER_GROUP
        e_log = jnp.where((lane >= first) & (lane < first + EXPERTS_PER_GROUP), logits, neg)
        v1 = jnp.max(e_log, axis=-1, keepdims=True)
        i1 = jnp.min(jnp.where(e_log == v1, lane, ROUTER_NO_LANE), axis=-1, keepdims=True)
        e_log2 = jnp.where(lane == i1, neg, e_log)
        v2 = jnp.max(e_log2, axis=-1, keepdims=True)
        i2 = jnp.min(jnp.where(e_log2 == v2, lane, ROUTER_NO_LANE), axis=-1, keepdims=True)
        e2 = jnp.exp(v2 - v1)
        w1 = p_group / (1.0 + e2)
        w2 = p_group * e2 / (1.0 + e2)
        combine = jnp.where(lane == i1, w1, 0.0) + jnp.where(lane == i2, w2, 0.0)

        xb = x.astype(BF16)
        hidden = []
        for e in range(EXPERTS_PER_GROUP):
            w_tok = jnp.sum(jnp.where(lane == first + e, combine, 0.0), axis=-1, keepdims=True)
            hg = _dot(xb, wg_ref[e])
            hu = _dot(xb, wu_ref[e])
            hidden.append((hg * jax.nn.sigmoid(hg) * hu * w_tok).astype(BF16))
        y = _dot(jnp.concatenate(hidden, axis=1), wd_ref[...])

        @pl.when(i >= 1)
        def _():
            wait_scatter(tile_valid_ref[jnp.maximum(i - 1, 0)])

        ybuf[...] = y
        n_valid = tile_valid_ref[i]
        start_scatter(n_valid)

        @pl.when(i == n_tiles - 1)
        def _():
            wait_scatter(n_valid)


def _experts(u2, gidx, w_r, b_r, w_g, w_u, w_d):
    n, d = u2.shape
    tm = MOE_ROWS
    d_e = w_g.shape[-1]
    tile_group, tile_valid, n_tiles, slot_token = _route_plan(gidx, tm)
    max_tiles = slot_token.shape[0]
    w_g = w_g.reshape(N_GROUPS, EXPERTS_PER_GROUP, d, d_e)
    w_u = w_u.reshape(N_GROUPS, EXPERTS_PER_GROUP, d, d_e)
    w_d = w_d.reshape(N_GROUPS, EXPERTS_PER_GROUP * d_e, d)
    grid_spec = pltpu.PrefetchScalarGridSpec(
        num_scalar_prefetch=3,
        grid=(max_tiles,),
        in_specs=[pl.BlockSpec((None, 1, tm), lambda i, tg, tv, nt: (i, 0, 0), memory_space=pltpu.SMEM),
                  pl.BlockSpec((None, 1, tm), lambda i, tg, tv, nt: (jnp.minimum(i + 1, max_tiles - 1), 0, 0),
                               memory_space=pltpu.SMEM),
                  pl.BlockSpec(memory_space=pl.ANY),
                  pl.BlockSpec(w_r.shape, lambda i, tg, tv, nt: (0, 0)),
                  pl.BlockSpec(b_r.shape, lambda i, tg, tv, nt: (0, 0)),
                  pl.BlockSpec((None, EXPERTS_PER_GROUP, d, d_e), lambda i, tg, tv, nt: (tg[i], 0, 0, 0)),
                  pl.BlockSpec((None, EXPERTS_PER_GROUP, d, d_e), lambda i, tg, tv, nt: (tg[i], 0, 0, 0)),
                  pl.BlockSpec((None, EXPERTS_PER_GROUP * d_e, d), lambda i, tg, tv, nt: (tg[i], 0, 0))],
        out_specs=pl.BlockSpec(memory_space=pl.ANY),
        scratch_shapes=[pltpu.VMEM((2, tm, d), F32),
                        pltpu.VMEM((tm, d), F32),
                        pltpu.SemaphoreType.DMA((2,)),
                        pltpu.SemaphoreType.DMA(())],
    )
    return pl.pallas_call(
        functools.partial(_experts_kernel, tm=tm),
        grid_spec=grid_spec,
        out_shape=jax.ShapeDtypeStruct((n, d), F32),
        compiler_params=_cparams(("arbitrary",)),
        name="experts",
    )(tile_group, tile_valid, n_tiles, slot_token, slot_token, u2, w_r, b_r, w_g, w_u, w_d)


def _ln2_kernel(x1_ref, y_ref, gate2_ref, ln_g_ref, ln_b_ref, o_ref):
    v = DEEPNORM_ALPHA * x1_ref[...] + gate2_ref[...] * y_ref[...]
    o_ref[...] = _layer_norm(v, ln_g_ref[...], ln_b_ref[...])


def _ln2(x1, y, mod4, ln_g, ln_b, seq):
    n, d = x1.shape
    tm = min(LN2_ROWS, seq)
    tiles_per_seq = seq // tm
    rows = pl.BlockSpec((tm, d), lambda i: (i, 0))
    vec = pl.BlockSpec((1, d), lambda i: (0, 0))
    return pl.pallas_call(
        _ln2_kernel,
        grid=(n // tm,),
        in_specs=[rows, rows, pl.BlockSpec((None, None, 1, d), lambda i: (i // tiles_per_seq, 5, 0, 0)), vec, vec],
        out_specs=rows,
        out_shape=jax.ShapeDtypeStruct((n, d), F32),
        compiler_params=_cparams(("arbitrary",)),
        name="ln2",
    )(x1, y, mod4, ln_g, ln_b)


def kernel(x, c, positions, w_ada, b_ada, w_in, mla_q_norm_g, w_q_up, mla_kv_norm_g, w_kv_up, w_branch_sb, w_branch_mla, w_out, ln1_g, ln1_b, w_router_group, b_router_group, w_router_expert, b_router_expert, w_exp_gate, w_exp_up, w_exp_down, ln2_g, ln2_b):
    bsz, seq, d = x.shape
    n = bsz * seq
    invf = _rope_inv_freq_tile()
    posf = positions.astype(F32).reshape(n, 1)
    tri = _suffix_sum_matrix()
    for l in range(w_ada.shape[0]):
        x2 = x.reshape(n, d)
        mod4 = _adaln_mod(c, w_ada[l], b_ada[l]).reshape(bsz, N_MOD, 1, d)
        weights = _prep_in_weights(w_in[l], w_q_up[l], w_kv_up[l])
        sb, qm, km, vm, gates = _in_proj(x2, mod4, posf, invf, weights,
                                         mla_q_norm_g[l].reshape(1, -1), mla_kv_norm_g[l].reshape(1, -1), seq)
        o_sb = _sb_attention(sb.reshape(bsz, seq, -1), tri).reshape(n, SB_WIDTH)
        o_mla = _mla_attention(qm.reshape(bsz, seq, -1), km.reshape(bsz, seq, -1),
                               vm.reshape(bsz, seq, -1)).reshape(n, -1)
        w_r = jnp.zeros((d, LANES), F32)
        w_r = w_r.at[:, :N_EXPERTS].set(w_router_expert[l]).at[:, N_EXPERTS:N_EXPERTS + N_GROUPS].set(w_router_group[l])
        b_r = jnp.zeros((1, LANES), F32)
        b_r = b_r.at[0, :N_EXPERTS].set(b_router_expert[l]).at[0, N_EXPERTS:N_EXPERTS + N_GROUPS].set(b_router_group[l])
        w_r = w_r.astype(BF16)
        x1, u2, gidx = _merge(o_sb, o_mla, gates, x2, mod4,
                              w_branch_sb[l].astype(BF16), w_branch_mla[l].astype(BF16), w_out[l].astype(BF16),
                              ln1_g[l].reshape(1, d), ln1_b[l].reshape(1, d), w_r, b_r, seq)
        y = _experts(u2, gidx.reshape(n), w_r, b_r, w_exp_gate[l].astype(BF16), w_exp_up[l].astype(BF16),
                     w_exp_down[l].astype(BF16))
        out = _ln2(x1, y, mod4, ln2_g[l].reshape(1, d), ln2_b[l].reshape(1, d), seq)
        x = out.reshape(bsz, seq, d)
    return x
```

```python
import functools
import math

import jax
import jax.numpy as jnp
import numpy as np
from jax import lax
from jax.experimental import pallas as pl
from jax.experimental.pallas import tpu as pltpu

F32 = jnp.float32
BF16 = jnp.bfloat16

D_MODEL = 1024
SB_HEADS = 8
SB_HEAD_DIM = 64
SB_WIDTH = SB_HEADS * SB_HEAD_DIM
MLA_HEADS = 8
MLA_NOPE_DIM = 64
MLA_ROPE_DIM = 32
MLA_V_DIM = 64
MLA_Q_RANK = 384
MLA_KV_RANK = 256
MLA_QK_DIM = MLA_NOPE_DIM + MLA_ROPE_DIM
MLA_WIDTH = MLA_HEADS * MLA_V_DIM
ROPE_THETA = 10000.0
N_GROUPS = 4
EXPERTS_PER_GROUP = 8
N_EXPERTS = N_GROUPS * EXPERTS_PER_GROUP
D_EXPERT = 256
DEPTH = 1
DEEPNORM_ALPHA = (2.0 * DEPTH) ** 0.25
LN_EPS = 1e-5
RMS_EPS = 1e-6
LOG2_E = math.log2(math.e)
MASK_BIAS = -1e30
N_MOD = 6

LANES = 128
MLA_HEAD_LANES = 128
VMEM_LIMIT_BYTES = 56 * 1024 * 1024

PROJ_ROWS = 512
ATTN_Q = 256
ATTN_K = 256
MERGE_ROWS = 512
MOE_ROWS = 512
LN2_ROWS = 512
DMA_ISSUE_UNROLL = 8
SB_PAIRS_PER_STEP = 2
MLA_HEADS_PER_STEP = 4
PIPELINE_UNROLL = 4


def _cparams(sem):
    return pltpu.CompilerParams(dimension_semantics=sem, vmem_limit_bytes=VMEM_LIMIT_BYTES)


def _dot(a, b):
    return jnp.dot(a, b, preferred_element_type=F32)


def _dot_t(a, b):
    return lax.dot_general(a, b, (((1,), (1,)), ((), ())), preferred_element_type=F32)


def _adaln_kernel(c_ref, w_ref, b_ref, o_ref):
    c = c_ref[...]
    ca = c * jax.nn.sigmoid(c)
    o_ref[...] = _dot(ca.astype(BF16), w_ref[...].astype(BF16)) + b_ref[...]


def _adaln_mod(c, w_ada, b_ada):
    bsz, d = c.shape
    return pl.pallas_call(
        _adaln_kernel,
        grid=(N_MOD,),
        in_specs=[pl.BlockSpec((bsz, d), lambda j: (0, 0)),
                  pl.BlockSpec((d, d), lambda j: (0, j)),
                  pl.BlockSpec((1, d), lambda j: (0, j))],
        out_specs=pl.BlockSpec((bsz, d), lambda j: (0, j)),
        out_shape=jax.ShapeDtypeStruct((bsz, N_MOD * d), F32),
        compiler_params=_cparams(("arbitrary",)),
        name="adaln_mod",
    )(c, w_ada, b_ada.reshape(1, N_MOD * d))


def _in_proj_kernel(x_ref, scale_ref, shift_ref, pos_ref, invf_ref,
                    w_sb_ref, w_dn_ref, w_kr_ref, w_g_ref, qg_ref, kvg_ref,
                    w_qu_ref, w_ku_ref, w_vu_ref,
                    sb_ref, qm_ref, km_ref, vm_ref, g_ref):
    u = (x_ref[...] * (1.0 + scale_ref[...]) + shift_ref[...]).astype(BF16)

    sb = _dot(u, w_sb_ref[...])
    sb_scale = LOG2_E / math.sqrt(SB_HEAD_DIM)
    sb_ref[:, :SB_WIDTH] = (sb[:, :SB_WIDTH] * sb_scale).astype(BF16)
    sb_ref[:, SB_WIDTH:] = sb[:, SB_WIDTH:].astype(BF16)

    g_ref[...] = _dot(u, w_g_ref[...]).astype(BF16)

    ang = pos_ref[...] * invf_ref[...]
    cos = jnp.cos(ang)
    sin = jnp.sin(ang)

    dn = _dot(u, w_dn_ref[...])
    q_dn = dn[:, :MLA_Q_RANK]
    kv_dn = dn[:, MLA_Q_RANK:]
    qn = q_dn * lax.rsqrt(jnp.mean(q_dn * q_dn, axis=-1, keepdims=True) + RMS_EPS) * qg_ref[...]
    kvn = kv_dn * lax.rsqrt(jnp.mean(kv_dn * kv_dn, axis=-1, keepdims=True) + RMS_EPS) * kvg_ref[...]
    qn = qn.astype(BF16)
    kvn = kvn.astype(BF16)

    width = MLA_HEADS * MLA_HEAD_LANES
    qu = _dot(qn, w_qu_ref[...])
    ku = _dot(kvn, w_ku_ref[...])
    vu = _dot(kvn, w_vu_ref[...])
    v_lane = lax.broadcasted_iota(jnp.int32, vu.shape, 1) % MLA_HEAD_LANES
    vm_ref[...] = jnp.where(v_lane < MLA_V_DIM, vu, 1.0).astype(BF16)
    kr = _dot(u, w_kr_ref[...])
    kr = kr[:, :LANES] * cos + kr[:, LANES:] * sin
    mla_scale = LOG2_E / math.sqrt(MLA_QK_DIM)
    for h in range(MLA_HEADS):
        sl = slice(h * MLA_HEAD_LANES, (h + 1) * MLA_HEAD_LANES)
        sl_rot = slice(width + h * MLA_HEAD_LANES, width + (h + 1) * MLA_HEAD_LANES)
        qm_ref[:, sl] = ((qu[:, sl] * cos + qu[:, sl_rot] * sin) * mla_scale).astype(BF16)
        km_ref[:, sl] = (ku[:, sl] + kr).astype(BF16)


def _rope_inv_freq_tile():
    inv_freq = 1.0 / (ROPE_THETA ** (jnp.arange(0, MLA_ROPE_DIM, 2, dtype=F32) / MLA_ROPE_DIM))
    tile = jnp.zeros((LANES,), F32)
    half = MLA_ROPE_DIM // 2
    tile = tile.at[MLA_NOPE_DIM:MLA_NOPE_DIM + half].set(inv_freq)
    tile = tile.at[MLA_NOPE_DIM + half:MLA_NOPE_DIM + 2 * half].set(inv_freq)
    return tile.reshape(1, LANES)


def _rotate_half_cols(w):
    half = MLA_ROPE_DIM // 2
    return jnp.concatenate([-w[..., half:], w[..., :half]], axis=-1)


def _prep_in_weights(w_in, w_q_up, w_kv_up):
    o = 0
    w_sb = w_in[:, o:o + 3 * SB_WIDTH]; o += 3 * SB_WIDTH
    w_dn = w_in[:, o:o + MLA_Q_RANK + MLA_KV_RANK]; o += MLA_Q_RANK + MLA_KV_RANK
    w_kr = w_in[:, o:o + MLA_ROPE_DIM]; o += MLA_ROPE_DIM
    w_g = w_in[:, o:]

    def rope_tile(w):
        z = jnp.zeros((w.shape[0], LANES), w.dtype)
        return z.at[:, MLA_NOPE_DIM:MLA_NOPE_DIM + MLA_ROPE_DIM].set(w)

    w_kr2 = jnp.concatenate([rope_tile(w_kr), rope_tile(_rotate_half_cols(w_kr))], axis=1)

    wq = w_q_up.reshape(MLA_Q_RANK, MLA_HEADS, MLA_QK_DIM)
    q_base = jnp.zeros((MLA_Q_RANK, MLA_HEADS, MLA_HEAD_LANES), w_q_up.dtype).at[:, :, :MLA_QK_DIM].set(wq)
    q_rot = jnp.zeros((MLA_Q_RANK, MLA_HEADS, MLA_HEAD_LANES), w_q_up.dtype)
    q_rot = q_rot.at[:, :, MLA_NOPE_DIM:MLA_QK_DIM].set(_rotate_half_cols(wq[:, :, MLA_NOPE_DIM:]))
    w_qu = jnp.concatenate([q_base.reshape(MLA_Q_RANK, -1), q_rot.reshape(MLA_Q_RANK, -1)], axis=1)

    wkv = w_kv_up.reshape(MLA_KV_RANK, MLA_HEADS, MLA_NOPE_DIM + MLA_V_DIM)
    w_ku = jnp.zeros((MLA_KV_RANK, MLA_HEADS, MLA_HEAD_LANES), w_kv_up.dtype)
    w_ku = w_ku.at[:, :, :MLA_NOPE_DIM].set(wkv[:, :, :MLA_NOPE_DIM]).reshape(MLA_KV_RANK, -1)
    w_vu = jnp.zeros((MLA_KV_RANK, MLA_HEADS, MLA_HEAD_LANES), w_kv_up.dtype)
    w_vu = w_vu.at[:, :, :MLA_V_DIM].set(wkv[:, :, MLA_NOPE_DIM:]).reshape(MLA_KV_RANK, -1)
    return tuple(w.astype(BF16) for w in (w_sb, w_dn, w_kr2, w_g, w_qu, w_ku, w_vu))


def _in_proj(x2, mod4, posf, invf, weights, qg, kvg, seq):
    n, d = x2.shape
    tm = min(PROJ_ROWS, seq)
    tiles_per_seq = seq // tm
    w_sb, w_dn, w_kr2, w_g, w_qu, w_ku, w_vu = weights

    def whole(a):
        return pl.BlockSpec(a.shape, lambda i: (0,) * a.ndim)

    def mod_spec(k):
        return pl.BlockSpec((None, None, 1, d), lambda i: (i // tiles_per_seq, k, 0, 0))

    def rows(width):
        return pl.BlockSpec((tm, width), lambda i: (i, 0))

    width = MLA_HEADS * MLA_HEAD_LANES
    return pl.pallas_call(
        _in_proj_kernel,
        grid=(n // tm,),
        in_specs=[rows(d), mod_spec(1), mod_spec(0), rows(1), whole(invf),
                  whole(w_sb), whole(w_dn), whole(w_kr2), whole(w_g), whole(qg), whole(kvg),
                  whole(w_qu), whole(w_ku), whole(w_vu)],
        out_specs=[rows(3 * SB_WIDTH), rows(width), rows(width), rows(width), rows(2 * d)],
        out_shape=[jax.ShapeDtypeStruct((n, 3 * SB_WIDTH), BF16),
                   jax.ShapeDtypeStruct((n, width), BF16),
                   jax.ShapeDtypeStruct((n, width), BF16),
                   jax.ShapeDtypeStruct((n, width), BF16),
                   jax.ShapeDtypeStruct((n, 2 * d), BF16)],
        compiler_params=_cparams(("arbitrary",)),
        name="in_proj",
    )(x2, mod4, mod4, posf, invf, w_sb, w_dn, w_kr2, w_g, qg, kvg, w_qu, w_ku, w_vu)


def _item_schedule(n_q):
    q_of, t_of = [], []
    for qi in range(n_q):
        for t in range(qi + 1):
            q_of.append(qi)
            t_of.append(t)
    return np.asarray(q_of, np.int32), np.asarray(t_of, np.int32)


def _causal_bias(tq, tk, strict):
    r = jnp.arange(tq)[:, None]
    c = jnp.arange(tk)[None, :]
    visible = (c < r) if strict else (c <= r)
    return jnp.stack([jnp.zeros((tq, tk), F32), jnp.where(visible, 0.0, MASK_BIAS).astype(F32)])


def _pipeline3(n, stage_a, stage_b, stage_c, newest_first):
    stage_a(0, 0)
    if n >= 2:
        stage_a(1, 1)
    stage_b(0, 0)

    def step(i, slot):
        stages = [lambda: stage_a(i, slot), lambda: stage_b(i - 1, 1 - slot), lambda: stage_c(i - 2, slot)]
        for run in (stages if newest_first else reversed(stages)):
            run()

    def body(k, carry):
        for j in range(PIPELINE_UNROLL):
            step(2 + PIPELINE_UNROLL * k + j, j % 2)
        return carry

    bodies = max(n - 2, 0) // PIPELINE_UNROLL
    lax.fori_loop(0, bodies, body, 0)
    for i in range(2 + bodies * PIPELINE_UNROLL, n):
        step(i, i % 2)
    if n >= 2:
        stage_c(n - 2, n % 2)
        stage_b(n - 1, (n - 1) % 2)
    stage_c(n - 1, (n - 1) % 2)


def _suffix_sum_matrix():
    j = jnp.arange(2 * LANES)[:, None] % LANES
    n = jnp.arange(2 * LANES)[None, :]
    return jnp.where(n < LANES, (j >= n), True).astype(F32)


def _sb_kernel(item_q_ref, item_t_ref, q_ref, k_ref, v_ref, tri_ref, bias_ref, o_ref,
               qh_ref, c_ref, acc_ref, z_ref, w_ref, *, tq, tk, n_pairs, n_items):
    lane = lax.broadcasted_iota(jnp.int32, (tq, LANES), 1)
    n_sub = tk // LANES
    n_chains = 2 * n_pairs

    for p in range(n_pairs):
        qp = q_ref[:, p * LANES:(p + 1) * LANES]
        q_lane = lax.broadcasted_iota(jnp.int32, qp.shape, 1)
        for h in range(2):
            head_lanes = (q_lane < SB_HEAD_DIM) if h == 0 else (q_lane >= SB_HEAD_DIM)
            qh_ref[2 * p + h] = jnp.where(head_lanes, qp, jnp.zeros_like(qp))
    c_ref[...] = jnp.zeros_like(c_ref)
    acc_ref[...] = jnp.zeros_like(acc_ref)

    def item(f):
        qi = item_q_ref[f]
        t = item_t_ref[f]
        q_rows = pl.ds(pl.multiple_of(qi * tq, tq), tq)
        k_rows = pl.ds(pl.multiple_of((qi - t) * tk, tk), tk)
        return q_rows, k_rows, t

    def pair_lanes(ci):
        return slice((ci // 2) * LANES, (ci // 2 + 1) * LANES)

    def scores(f, slot):
        q_rows, k_rows, t = item(f)
        bias = bias_ref[(t == 0).astype(jnp.int32)]
        for ci in range(n_chains):
            z_ref[slot, ci] = _dot_t(qh_ref[ci, q_rows, :], k_ref[k_rows, pair_lanes(ci)]) + bias

    def weights(f, slot):
        _, _, t = item(f)
        keep = jnp.where(t == 0, 0.0, 1.0)
        tri = tri_ref[...]
        sign = jnp.uint32(0x80000000)
        bf16_bits = jnp.uint32(0xFFFF0000)
        for ci in range(n_chains):
            c = c_ref[ci] * keep
            for s in reversed(range(n_sub)):
                z = z_ref[slot, ci, :, s * LANES:(s + 1) * LANES]
                neg_abs = pltpu.bitcast(pltpu.bitcast(z, jnp.uint32) | sign, F32)
                sp = jnp.maximum(z, 0.0) + jnp.log2(1.0 + jnp.exp2(neg_abs))
                hi = pltpu.bitcast(pltpu.bitcast(sp, jnp.uint32) & bf16_bits, F32)
                r = _dot(jnp.concatenate([hi, sp - hi], axis=1), tri)
                w_ref[slot, ci, :, s * LANES:(s + 1) * LANES] = jnp.exp2(z - (c + r[:, :LANES])).astype(BF16)
                c = c + r[:, LANES:]
            c_ref[ci] = c

    def values(f, slot):
        q_rows, k_rows, t = item(f)
        keep = jnp.where(t == 0, 0.0, 1.0)
        for p in range(n_pairs):
            accs = []
            for ci in (2 * p, 2 * p + 1):
                acc = acc_ref[ci] * keep + _dot(w_ref[slot, ci], v_ref[k_rows, pair_lanes(ci)])
                acc_ref[ci] = acc
                accs.append(acc)
            o_ref[q_rows, p * LANES:(p + 1) * LANES] = jnp.where(lane < SB_HEAD_DIM, accs[0], accs[1]).astype(o_ref.dtype)

    _pipeline3(n_items, scores, weights, values, newest_first=True)


def _sb_attention(sb3, tri):
    bsz, seq, _ = sb3.shape
    tq = min(ATTN_Q, seq)
    tk = tq
    n_pairs = SB_PAIRS_PER_STEP
    width = n_pairs * LANES
    groups = SB_WIDTH // width
    item_q, item_t = _item_schedule(seq // tq)
    bias = _causal_bias(tq, tk, strict=True)

    def seq_block(first_block):
        return pl.BlockSpec((None, seq, width), lambda b, g, iq, it: (b, 0, first_block + g))

    grid_spec = pltpu.PrefetchScalarGridSpec(
        num_scalar_prefetch=2,
        grid=(bsz, groups),
        in_specs=[seq_block(0), seq_block(groups), seq_block(2 * groups),
                  pl.BlockSpec(tri.shape, lambda b, g, iq, it: (0, 0)),
                  pl.BlockSpec(bias.shape, lambda b, g, iq, it: (0, 0, 0))],
        out_specs=seq_block(0),
        scratch_shapes=[pltpu.VMEM((2 * n_pairs, seq, LANES), BF16),
                        pltpu.VMEM((2 * n_pairs, tq, LANES), F32),
                        pltpu.VMEM((2 * n_pairs, tq, LANES), F32),
                        pltpu.VMEM((2, 2 * n_pairs, tq, tk), F32),
                        pltpu.VMEM((2, 2 * n_pairs, tq, tk), BF16)],
    )
    return pl.pallas_call(
        functools.partial(_sb_kernel, tq=tq, tk=tk, n_pairs=n_pairs, n_items=len(item_q)),
        grid_spec=grid_spec,
        out_shape=jax.ShapeDtypeStruct((bsz, seq, SB_WIDTH), BF16),
        compiler_params=_cparams(("arbitrary", "arbitrary")),
        name="sb_attn",
    )(item_q, item_t, sb3, sb3, sb3, tri, bias)


def _mla_kernel(item_q_ref, item_t_ref, q_ref, k_ref, v_ref, bias_ref, o_ref,
                m_ref, s_ref, p_ref, alpha_ref, *, tq, tk, n_heads, n_items):
    n_sub = tk // LANES
    m_ref[...] = jnp.zeros_like(m_ref)
    o_ref[...] = jnp.zeros_like(o_ref)

    def item(f):
        qi = item_q_ref[f]
        t = item_t_ref[f]
        q_rows = pl.ds(pl.multiple_of(qi * tq, tq), tq)
        k_rows = pl.ds(pl.multiple_of((qi - t) * tk, tk), tk)
        return q_rows, k_rows, t

    def head_lanes(ci):
        return slice(ci * MLA_HEAD_LANES, (ci + 1) * MLA_HEAD_LANES)

    def scores(f, slot):
        q_rows, k_rows, t = item(f)
        bias = bias_ref[(t == 0).astype(jnp.int32)]
        for ci in range(n_heads):
            s_ref[slot, ci] = _dot_t(q_ref[q_rows, head_lanes(ci)], k_ref[k_rows, head_lanes(ci)]) + bias

    def numerators(f, slot):
        _, _, t = item(f)
        for ci in range(n_heads):
            row_max = jnp.max(s_ref[slot, ci], axis=-1, keepdims=True)
            m = jnp.where(t == 0, -jnp.inf, m_ref[ci])
            m_new = jnp.maximum(m, jnp.broadcast_to(row_max, (tq, LANES)))
            alpha_ref[slot, ci] = jnp.exp2(m - m_new)
            for j in range(n_sub):
                sl = slice(j * LANES, (j + 1) * LANES)
                p_ref[slot, ci, :, sl] = jnp.exp2(s_ref[slot, ci, :, sl] - m_new).astype(BF16)
            m_ref[ci] = m_new

    def values(f, slot):
        q_rows, k_rows, _ = item(f)
        for ci in range(n_heads):
            acc = o_ref[q_rows, head_lanes(ci)]
            o_ref[q_rows, head_lanes(ci)] = (alpha_ref[slot, ci] * acc
                                             + _dot(p_ref[slot, ci], v_ref[k_rows, head_lanes(ci)]))

    _pipeline3(n_items, scores, numerators, values, newest_first=False)


def _mla_attention(qm3, km3, vm3):
    bsz, seq, _ = qm3.shape
    tq = min(ATTN_Q, seq)
    tk = tq
    n_heads = MLA_HEADS_PER_STEP
    groups = MLA_HEADS // n_heads
    width = n_heads * MLA_HEAD_LANES
    item_q, item_t = _item_schedule(seq // tq)
    bias = _causal_bias(tq, tk, strict=False)
    seq_block = pl.BlockSpec((None, seq, width), lambda b, g, iq, it: (b, 0, g))
    grid_spec = pltpu.PrefetchScalarGridSpec(
        num_scalar_prefetch=2,
        grid=(bsz, groups),
        in_specs=[seq_block, seq_block, seq_block, pl.BlockSpec(bias.shape, lambda b, g, iq, it: (0, 0, 0))],
        out_specs=seq_block,
        scratch_shapes=[pltpu.VMEM((n_heads, tq, LANES), F32),
                        pltpu.VMEM((2, n_heads, tq, tk), F32),
                        pltpu.VMEM((2, n_heads, tq, tk), BF16),
                        pltpu.VMEM((2, n_heads, tq, LANES), F32)],
    )
    return pl.pallas_call(
        functools.partial(_mla_kernel, tq=tq, tk=tk, n_heads=n_heads, n_items=len(item_q)),
        grid_spec=grid_spec,
        out_shape=jax.ShapeDtypeStruct((bsz, seq, MLA_HEADS * MLA_HEAD_LANES), F32),
        compiler_params=_cparams(("arbitrary", "arbitrary")),
        name="mla_attn",
    )(item_q, item_t, qm3, km3, vm3, bias)


def _layer_norm(v, g, b):
    mu = jnp.mean(v, axis=-1, keepdims=True)
    dv = v - mu
    var = jnp.mean(dv * dv, axis=-1, keepdims=True)
    return dv * lax.rsqrt(var + LN_EPS) * g + b


ROUTER_NO_LANE = 1 << 20


def _router_logits(u2, wr_ref, br_ref):
    return _dot(u2.astype(BF16), wr_ref[...]) + br_ref[...]


def _group_lanes(lane):
    return (lane >= N_EXPERTS) & (lane < N_EXPERTS + N_GROUPS)


def _merge_kernel(osb_ref, omla_ref, g_ref, x_ref, gate1_ref, scale2_ref, shift2_ref,
                  wsb_ref, wmla_ref, wout_ref, ln_g_ref, ln_b_ref, wr_ref, br_ref,
                  x1_ref, u2_ref, gidx_ref):
    d = x_ref.shape[-1]
    y_sb = _dot(osb_ref[...], wsb_ref[...])
    lane = lax.broadcasted_iota(jnp.int32, (osb_ref.shape[0], LANES), 1)
    o_mla = []
    for p in range(MLA_HEADS // 2):
        res = []
        for h in range(2):
            acc = omla_ref[:, (2 * p + h) * MLA_HEAD_LANES:(2 * p + h + 1) * MLA_HEAD_LANES]
            res.append(acc / pltpu.roll(acc, MLA_V_DIM, 1))
        o_mla.append(jnp.where(lane < MLA_V_DIM, res[0], pltpu.roll(res[1], MLA_V_DIM, 1)).astype(BF16))
    y_mla = _dot(jnp.concatenate(o_mla, axis=1), wmla_ref[...])
    gates = g_ref[...].astype(F32)
    mixed = jax.nn.sigmoid(gates[:, :d]) * y_sb + jax.nn.sigmoid(gates[:, d:]) * y_mla
    attn = _dot(mixed.astype(BF16), wout_ref[...])
    x1 = _layer_norm(DEEPNORM_ALPHA * x_ref[...] + gate1_ref[...] * attn, ln_g_ref[...], ln_b_ref[...])
    x1_ref[...] = x1
    u2 = x1 * (1.0 + scale2_ref[...]) + shift2_ref[...]
    u2_ref[...] = u2
    logits = _router_logits(u2, wr_ref, br_ref)
    lane = lax.broadcasted_iota(jnp.int32, logits.shape, 1)
    g_log = jnp.where(_group_lanes(lane), logits, -jnp.inf)
    g_max = jnp.max(g_log, axis=-1, keepdims=True)
    gidx_ref[...] = jnp.min(jnp.where(g_log == g_max, lane, ROUTER_NO_LANE), axis=-1, keepdims=True) - N_EXPERTS


def _merge(o_sb, o_mla, gates, x2, mod4, w_bsb, w_bmla, w_out, ln_g, ln_b, w_r, b_r, seq):
    n, d = x2.shape
    tm = min(MERGE_ROWS, seq)
    tiles_per_seq = seq // tm

    def whole(a):
        return pl.BlockSpec(a.shape, lambda i: (0,) * a.ndim)

    def mod_spec(k):
        return pl.BlockSpec((None, None, 1, d), lambda i: (i // tiles_per_seq, k, 0, 0))

    def rows(width):
        return pl.BlockSpec((tm, width), lambda i: (i, 0))

    return pl.pallas_call(
        _merge_kernel,
        grid=(n // tm,),
        in_specs=[rows(SB_WIDTH), rows(MLA_HEADS * MLA_HEAD_LANES), rows(2 * d), rows(d),
                  mod_spec(2), mod_spec(4), mod_spec(3),
                  whole(w_bsb), whole(w_bmla), whole(w_out), whole(ln_g), whole(ln_b), whole(w_r), whole(b_r)],
        out_specs=[rows(d), rows(d), rows(1)],
        out_shape=[jax.ShapeDtypeStruct((n, d), F32),
                   jax.ShapeDtypeStruct((n, d), F32),
                   jax.ShapeDtypeStruct((n, 1), jnp.int32)],
        compiler_params=_cparams(("arbitrary",)),
        name="merge",
    )(o_sb, o_mla, gates, x2, mod4, mod4, mod4, w_bsb, w_bmla, w_out, ln_g, ln_b, w_r, b_r)


def _route_plan(gidx, tm):
    n = gidx.shape[0]
    max_tiles = n // tm + N_GROUPS
    order = jnp.argsort(gidx, stable=True).astype(jnp.int32)
    counts = jnp.sum(gidx[:, None] == jnp.arange(N_GROUPS, dtype=jnp.int32)[None, :], axis=0).astype(jnp.int32)
    tiles = (counts + tm - 1) // tm
    tile_end = jnp.cumsum(tiles)
    tile_start = tile_end - tiles
    run_start = jnp.cumsum(counts) - counts
    tile_ids = jnp.arange(max_tiles, dtype=jnp.int32)
    n_tiles = tile_end[-1]
    tile_group = jnp.minimum(jnp.sum(tile_ids[:, None] >= tile_end[None, :], axis=1), N_GROUPS - 1).astype(jnp.int32)
    rows_before = (tile_ids - tile_start[tile_group]) * tm
    tile_valid = jnp.clip(counts[tile_group] - rows_before, 0, tm)
    tile_valid = jnp.where(tile_ids < n_tiles, tile_valid, 0).astype(jnp.int32)
    row = jnp.arange(tm, dtype=jnp.int32)[None, :]
    sorted_pos = (run_start[tile_group] + rows_before)[:, None] + row
    slot_token = jnp.where(row < tile_valid[:, None], order[jnp.clip(sorted_pos, 0, n - 1)], 0)
    return tile_group, tile_valid, n_tiles.reshape(1).astype(jnp.int32), slot_token.reshape(max_tiles, 1, tm)


def _experts_kernel(tile_group_ref, tile_valid_ref, n_tiles_ref, tok_ref, tok_next_ref,
                    u2_hbm, wr_ref, br_ref, wg_ref, wu_ref, wd_ref, y_hbm,
                    xbuf, ybuf, gather_sem, scatter_sem, *, tm):
    i = pl.program_id(0)
    n_tiles = n_tiles_ref[0]
    buf = i % 2

    def gather_row(tok, r, b):
        return pltpu.make_async_copy(u2_hbm.at[pl.ds(tok, 1)], xbuf.at[b, pl.ds(r, 1)], gather_sem.at[b])

    def scatter_row(tok, r):
        return pltpu.make_async_copy(ybuf.at[pl.ds(r, 1)], y_hbm.at[pl.ds(tok, 1)], scatter_sem)

    def start_gather(idx_ref, b):
        def body(r, carry):
            gather_row(idx_ref[0, r], r, b).start()
            return carry
        lax.fori_loop(0, tm, body, 0, unroll=DMA_ISSUE_UNROLL)

    def start_scatter(n_rows):
        def body(r, carry):
            scatter_row(tok_ref[0, r], r).start()
            return carry

        @pl.when(n_rows == tm)
        def _():
            lax.fori_loop(0, tm, body, 0, unroll=DMA_ISSUE_UNROLL)

        @pl.when(n_rows != tm)
        def _():
            lax.fori_loop(0, n_rows, body, 0)

    def wait_scatter(n_rows):
        def body(r, carry):
            scatter_row(0, r).wait()
            return carry

        @pl.when(n_rows == tm)
        def _():
            pltpu.make_async_copy(ybuf, ybuf, scatter_sem).wait()

        @pl.when(n_rows != tm)
        def _():
            lax.fori_loop(0, n_rows, body, 0)

    @pl.when(i == 0)
    def _():
        start_gather(tok_ref, 0)

    @pl.when(i < n_tiles)
    def _():
        pltpu.make_async_copy(xbuf.at[buf], xbuf.at[buf], gather_sem.at[buf]).wait()

        @pl.when(i + 1 < n_tiles)
        def _():
            start_gather(tok_next_ref, 1 - buf)

        group = tile_group_ref[i]
        x = xbuf[buf]
        logits = _router_logits(x, wr_ref, br_ref)
        lane = lax.broadcasted_iota(jnp.int32, logits.shape, 1)
        neg = -jnp.inf
        g_log = jnp.where(_group_lanes(lane), logits, neg)
        g_max = jnp.max(g_log, axis=-1, keepdims=True)
        p_group = 1.0 / jnp.sum(jnp.exp(g_log - g_max), axis=-1, keepdims=True)
        first = group * EXPERTS_PER_GROUP
        e_log = jnp.where((lane >= first) & (lane < first + EXPERTS_PER_GROUP), logits, neg)
        v1 = jnp.max(e_log, axis=-1, keepdims=True)
        i1 = jnp.min(jnp.where(e_log == v1, lane, ROUTER_NO_LANE), axis=-1, keepdims=True)
        e_log2 = jnp.where(lane == i1, neg, e_log)
        v2 = jnp.max(e_log2, axis=-1, keepdims=True)
        i2 = jnp.min(jnp.where(e_log2 == v2, lane, ROUTER_NO_LANE), axis=-1, keepdims=True)
        e2 = jnp.exp(v2 - v1)
        w1 = p_group / (1.0 + e2)
        w2 = p_group * e2 / (1.0 + e2)
        combine = jnp.where(lane == i1, w1, 0.0) + jnp.where(lane == i2, w2, 0.0)

        xb = x.astype(BF16)
        hidden = []
        for e in range(EXPERTS_PER_GROUP):
            w_tok = jnp.sum(jnp.where(lane == first + e, combine, 0.0), axis=-1, keepdims=True)
            hg = _dot(xb, wg_ref[e])
            hu = _dot(xb, wu_ref[e])
            hidden.append((hg * jax.nn.sigmoid(hg) * hu * w_tok).astype(BF16))
        y = _dot(jnp.concatenate(hidden, axis=1), wd_ref[...])

        @pl.when(i >= 1)
        def _():
            wait_scatter(tile_valid_ref[jnp.maximum(i - 1, 0)])

        ybuf[...] = y
        n_valid = tile_valid_ref[i]
        start_scatter(n_valid)

        @pl.when(i == n_tiles - 1)
        def _():
            wait_scatter(n_valid)


def _experts(u2, gidx, w_r, b_r, w_g, w_u, w_d):
    n, d = u2.shape
    tm = MOE_ROWS
    d_e = w_g.shape[-1]
    tile_group, tile_valid, n_tiles, slot_token = _route_plan(gidx, tm)
    max_tiles = slot_token.shape[0]
    w_g = w_g.reshape(N_GROUPS, EXPERTS_PER_GROUP, d, d_e)
    w_u = w_u.reshape(N_GROUPS, EXPERTS_PER_GROUP, d, d_e)
    w_d = w_d.reshape(N_GROUPS, EXPERTS_PER_GROUP * d_e, d)
    grid_spec = pltpu.PrefetchScalarGridSpec(
        num_scalar_prefetch=3,
        grid=(max_tiles,),
        in_specs=[pl.BlockSpec((None, 1, tm), lambda i, tg, tv, nt: (i, 0, 0), memory_space=pltpu.SMEM),
                  pl.BlockSpec((None, 1, tm), lambda i, tg, tv, nt: (jnp.minimum(i + 1, max_tiles - 1), 0, 0),
                               memory_space=pltpu.SMEM),
                  pl.BlockSpec(memory_space=pl.ANY),
                  pl.BlockSpec(w_r.shape, lambda i, tg, tv, nt: (0, 0)),
                  pl.BlockSpec(b_r.shape, lambda i, tg, tv, nt: (0, 0)),
                  pl.BlockSpec((None, EXPERTS_PER_GROUP, d, d_e), lambda i, tg, tv, nt: (tg[i], 0, 0, 0)),
                  pl.BlockSpec((None, EXPERTS_PER_GROUP, d, d_e), lambda i, tg, tv, nt: (tg[i], 0, 0, 0)),
                  pl.BlockSpec((None, EXPERTS_PER_GROUP * d_e, d), lambda i, tg, tv, nt: (tg[i], 0, 0))],
        out_specs=pl.BlockSpec(memory_space=pl.ANY),
        scratch_shapes=[pltpu.VMEM((2, tm, d), F32),
                        pltpu.VMEM((tm, d), F32),
                        pltpu.SemaphoreType.DMA((2,)),
                        pltpu.SemaphoreType.DMA(())],
    )
    return pl.pallas_call(
        functools.partial(_experts_kernel, tm=tm),
        grid_spec=grid_spec,
        out_shape=jax.ShapeDtypeStruct((n, d), F32),
        compiler_params=_cparams(("arbitrary",)),
        name="experts",
    )(tile_group, tile_valid, n_tiles, slot_token, slot_token, u2, w_r, b_r, w_g, w_u, w_d)


def _ln2_kernel(x1_ref, y_ref, gate2_ref, ln_g_ref, ln_b_ref, o_ref):
    v = DEEPNORM_ALPHA * x1_ref[...] + gate2_ref[...] * y_ref[...]
    o_ref[...] = _layer_norm(v, ln_g_ref[...], ln_b_ref[...])


def _ln2(x1, y, mod4, ln_g, ln_b, seq):
    n, d = x1.shape
    tm = min(LN2_ROWS, seq)
    tiles_per_seq = seq // tm
    rows = pl.BlockSpec((tm, d), lambda i: (i, 0))
    vec = pl.BlockSpec((1, d), lambda i: (0, 0))
    return pl.pallas_call(
        _ln2_kernel,
        grid=(n // tm,),
        in_specs=[rows, rows, pl.BlockSpec((None, None, 1, d), lambda i: (i // tiles_per_seq, 5, 0, 0)), vec, vec],
        out_specs=rows,
        out_shape=jax.ShapeDtypeStruct((n, d), F32),
        compiler_params=_cparams(("arbitrary",)),
        name="ln2",
    )(x1, y, mod4, ln_g, ln_b)


def kernel(x, c, positions, w_ada, b_ada, w_in, mla_q_norm_g, w_q_up, mla_kv_norm_g, w_kv_up, w_branch_sb, w_branch_mla, w_out, ln1_g, ln1_b, w_router_group, b_router_group, w_router_expert, b_router_expert, w_exp_gate, w_exp_up, w_exp_down, ln2_g, ln2_b):
    bsz, seq, d = x.shape
    n = bsz * seq
    invf = _rope_inv_freq_tile()
    posf = positions.astype(F32).reshape(n, 1)
    tri = _suffix_sum_matrix()
    for l in range(w_ada.shape[0]):
        x2 = x.reshape(n, d)
        mod4 = _adaln_mod(c, w_ada[l], b_ada[l]).reshape(bsz, N_MOD, 1, d)
        weights = _prep_in_weights(w_in[l], w_q_up[l], w_kv_up[l])
        sb, qm, km, vm, gates = _in_proj(x2, mod4, posf, invf, weights,
                                         mla_q_norm_g[l].reshape(1, -1), mla_kv_norm_g[l].reshape(1, -1), seq)
        o_sb = _sb_attention(sb.reshape(bsz, seq, -1), tri).reshape(n, SB_WIDTH)
        o_mla = _mla_attention(qm.reshape(bsz, seq, -1), km.reshape(bsz, seq, -1),
                               vm.reshape(bsz, seq, -1)).reshape(n, -1)
        w_r = jnp.zeros((d, LANES), F32)
        w_r = w_r.at[:, :N_EXPERTS].set(w_router_expert[l]).at[:, N_EXPERTS:N_EXPERTS + N_GROUPS].set(w_router_group[l])
        b_r = jnp.zeros((1, LANES), F32)
        b_r = b_r.at[0, :N_EXPERTS].set(b_router_expert[l]).at[0, N_EXPERTS:N_EXPERTS + N_GROUPS].set(b_router_group[l])
        w_r = w_r.astype(BF16)
        x1, u2, gidx = _merge(o_sb, o_mla, gates, x2, mod4,
                              w_branch_sb[l].astype(BF16), w_branch_mla[l].astype(BF16), w_out[l].astype(BF16),
                              ln1_g[l].reshape(1, d), ln1_b[l].reshape(1, d), w_r, b_r, seq)
        y = _experts(u2, gidx.reshape(n), w_r, b_r, w_exp_gate[l].astype(BF16), w_exp_up[l].astype(BF16),
                     w_exp_down[l].astype(BF16))
        out = _ln2(x1, y, mod4, ln2_g[l].reshape(1, d), ln2_b[l].reshape(1, d), seq)
        x = out.reshape(bsz, seq, d)
    return x
```

```python
import functools
import math

import jax
import jax.numpy as jnp
import numpy as np
from jax import lax
from jax.experimental import pallas as pl
from jax.experimental.pallas import tpu as pltpu

F32 = jnp.float32
BF16 = jnp.bfloat16

D_MODEL = 1024
SB_HEADS = 8
SB_HEAD_DIM = 64
SB_WIDTH = SB_HEADS * SB_HEAD_DIM
MLA_HEADS = 8
MLA_NOPE_DIM = 64
MLA_ROPE_DIM = 32
MLA_V_DIM = 64
MLA_Q_RANK = 384
MLA_KV_RANK = 256
MLA_QK_DIM = MLA_NOPE_DIM + MLA_ROPE_DIM
MLA_WIDTH = MLA_HEADS * MLA_V_DIM
ROPE_THETA = 10000.0
N_GROUPS = 4
EXPERTS_PER_GROUP = 8
N_EXPERTS = N_GROUPS * EXPERTS_PER_GROUP
D_EXPERT = 256
DEPTH = 1
DEEPNORM_ALPHA = (2.0 * DEPTH) ** 0.25
LN_EPS = 1e-5
RMS_EPS = 1e-6
LOG2_E = math.log2(math.e)
MASK_BIAS = -1e30
N_MOD = 6

LANES = 128
MLA_HEAD_LANES = 128
VMEM_LIMIT_BYTES = 56 * 1024 * 1024

PROJ_ROWS = 512
ATTN_Q = 256
MLA_ATTN_K = 512
MERGE_ROWS = 512
MOE_ROWS = 512
LN2_ROWS = 512
DMA_ISSUE_UNROLL = 8
SB_PAIRS_PER_STEP = 2
MLA_HEADS_PER_STEP = 4
PIPELINE_UNROLL = 4


def _cparams(sem):
    return pltpu.CompilerParams(dimension_semantics=sem, vmem_limit_bytes=VMEM_LIMIT_BYTES)


def _dot(a, b):
    return jnp.dot(a, b, preferred_element_type=F32)


def _dot_t(a, b):
    return lax.dot_general(a, b, (((1,), (1,)), ((), ())), preferred_element_type=F32)


def _adaln_kernel(c_ref, w_ref, b_ref, o_ref):
    c = c_ref[...]
    ca = c * jax.nn.sigmoid(c)
    o_ref[...] = _dot(ca.astype(BF16), w_ref[...].astype(BF16)) + b_ref[...]


def _adaln_mod(c, w_ada, b_ada):
    bsz, d = c.shape
    return pl.pallas_call(
        _adaln_kernel,
        grid=(N_MOD,),
        in_specs=[pl.BlockSpec((bsz, d), lambda j: (0, 0)),
                  pl.BlockSpec((d, d), lambda j: (0, j)),
                  pl.BlockSpec((1, d), lambda j: (0, j))],
        out_specs=pl.BlockSpec((bsz, d), lambda j: (0, j)),
        out_shape=jax.ShapeDtypeStruct((bsz, N_MOD * d), F32),
        compiler_params=_cparams(("arbitrary",)),
        name="adaln_mod",
    )(c, w_ada, b_ada.reshape(1, N_MOD * d))


def _in_proj_kernel(x_ref, scale_ref, shift_ref, pos_ref, invf_ref,
                    w_sb_ref, w_dn_ref, w_kr_ref, w_g_ref, qg_ref, kvg_ref,
                    w_qu_ref, w_ku_ref, w_vu_ref,
                    sb_ref, qm_ref, km_ref, vm_ref, g_ref):
    u = (x_ref[...] * (1.0 + scale_ref[...]) + shift_ref[...]).astype(BF16)

    sb = _dot(u, w_sb_ref[...])
    sb_scale = LOG2_E / math.sqrt(SB_HEAD_DIM)
    sb_ref[:, :SB_WIDTH] = (sb[:, :SB_WIDTH] * sb_scale).astype(BF16)
    sb_ref[:, SB_WIDTH:] = sb[:, SB_WIDTH:].astype(BF16)

    g_ref[...] = _dot(u, w_g_ref[...]).astype(BF16)

    ang = pos_ref[...] * invf_ref[...]
    cos = jnp.cos(ang)
    sin = jnp.sin(ang)

    dn = _dot(u, w_dn_ref[...])
    q_dn = dn[:, :MLA_Q_RANK]
    kv_dn = dn[:, MLA_Q_RANK:]
    qn = q_dn * lax.rsqrt(jnp.mean(q_dn * q_dn, axis=-1, keepdims=True) + RMS_EPS) * qg_ref[...]
    kvn = kv_dn * lax.rsqrt(jnp.mean(kv_dn * kv_dn, axis=-1, keepdims=True) + RMS_EPS) * kvg_ref[...]
    qn = qn.astype(BF16)
    kvn = kvn.astype(BF16)

    width = MLA_HEADS * MLA_HEAD_LANES
    qu = _dot(qn, w_qu_ref[...])
    ku = _dot(kvn, w_ku_ref[...])
    vu = _dot(kvn, w_vu_ref[...])
    v_lane = lax.broadcasted_iota(jnp.int32, vu.shape, 1) % MLA_HEAD_LANES
    vm_ref[...] = jnp.where(v_lane < MLA_V_DIM, vu, 1.0).astype(BF16)
    kr = _dot(u, w_kr_ref[...])
    kr = kr[:, :LANES] * cos + kr[:, LANES:] * sin
    mla_scale = LOG2_E / math.sqrt(MLA_QK_DIM)
    for h in range(MLA_HEADS):
        sl = slice(h * MLA_HEAD_LANES, (h + 1) * MLA_HEAD_LANES)
        sl_rot = slice(width + h * MLA_HEAD_LANES, width + (h + 1) * MLA_HEAD_LANES)
        qm_ref[:, sl] = ((qu[:, sl] * cos + qu[:, sl_rot] * sin) * mla_scale).astype(BF16)
        km_ref[:, sl] = (ku[:, sl] + kr).astype(BF16)


def _rope_inv_freq_tile():
    inv_freq = 1.0 / (ROPE_THETA ** (jnp.arange(0, MLA_ROPE_DIM, 2, dtype=F32) / MLA_ROPE_DIM))
    tile = jnp.zeros((LANES,), F32)
    half = MLA_ROPE_DIM // 2
    tile = tile.at[MLA_NOPE_DIM:MLA_NOPE_DIM + half].set(inv_freq)
    tile = tile.at[MLA_NOPE_DIM + half:MLA_NOPE_DIM + 2 * half].set(inv_freq)
    return tile.reshape(1, LANES)


def _rotate_half_cols(w):
    half = MLA_ROPE_DIM // 2
    return jnp.concatenate([-w[..., half:], w[..., :half]], axis=-1)


def _prep_in_weights(w_in, w_q_up, w_kv_up):
    o = 0
    w_sb = w_in[:, o:o + 3 * SB_WIDTH]; o += 3 * SB_WIDTH
    w_dn = w_in[:, o:o + MLA_Q_RANK + MLA_KV_RANK]; o += MLA_Q_RANK + MLA_KV_RANK
    w_kr = w_in[:, o:o + MLA_ROPE_DIM]; o += MLA_ROPE_DIM
    w_g = w_in[:, o:]

    def rope_tile(w):
        z = jnp.zeros((w.shape[0], LANES), w.dtype)
        return z.at[:, MLA_NOPE_DIM:MLA_NOPE_DIM + MLA_ROPE_DIM].set(w)

    w_kr2 = jnp.concatenate([rope_tile(w_kr), rope_tile(_rotate_half_cols(w_kr))], axis=1)

    wq = w_q_up.reshape(MLA_Q_RANK, MLA_HEADS, MLA_QK_DIM)
    q_base = jnp.zeros((MLA_Q_RANK, MLA_HEADS, MLA_HEAD_LANES), w_q_up.dtype).at[:, :, :MLA_QK_DIM].set(wq)
    q_rot = jnp.zeros((MLA_Q_RANK, MLA_HEADS, MLA_HEAD_LANES), w_q_up.dtype)
    q_rot = q_rot.at[:, :, MLA_NOPE_DIM:MLA_QK_DIM].set(_rotate_half_cols(wq[:, :, MLA_NOPE_DIM:]))
    w_qu = jnp.concatenate([q_base.reshape(MLA_Q_RANK, -1), q_rot.reshape(MLA_Q_RANK, -1)], axis=1)

    wkv = w_kv_up.reshape(MLA_KV_RANK, MLA_HEADS, MLA_NOPE_DIM + MLA_V_DIM)
    w_ku = jnp.zeros((MLA_KV_RANK, MLA_HEADS, MLA_HEAD_LANES), w_kv_up.dtype)
    w_ku = w_ku.at[:, :, :MLA_NOPE_DIM].set(wkv[:, :, :MLA_NOPE_DIM]).reshape(MLA_KV_RANK, -1)
    w_vu = jnp.zeros((MLA_KV_RANK, MLA_HEADS, MLA_HEAD_LANES), w_kv_up.dtype)
    w_vu = w_vu.at[:, :, :MLA_V_DIM].set(wkv[:, :, MLA_NOPE_DIM:]).reshape(MLA_KV_RANK, -1)
    return tuple(w.astype(BF16) for w in (w_sb, w_dn, w_kr2, w_g, w_qu, w_ku, w_vu))


def _in_proj(x2, mod4, posf, invf, weights, qg, kvg, seq):
    n, d = x2.shape
    tm = min(PROJ_ROWS, seq)
    tiles_per_seq = seq // tm
    w_sb, w_dn, w_kr2, w_g, w_qu, w_ku, w_vu = weights

    def whole(a):
        return pl.BlockSpec(a.shape, lambda i: (0,) * a.ndim)

    def mod_spec(k):
        return pl.BlockSpec((None, None, 1, d), lambda i: (i // tiles_per_seq, k, 0, 0))

    def rows(width):
        return pl.BlockSpec((tm, width), lambda i: (i, 0))

    width = MLA_HEADS * MLA_HEAD_LANES
    return pl.pallas_call(
        _in_proj_kernel,
        grid=(n // tm,),
        in_specs=[rows(d), mod_spec(1), mod_spec(0), rows(1), whole(invf),
                  whole(w_sb), whole(w_dn), whole(w_kr2), whole(w_g), whole(qg), whole(kvg),
                  whole(w_qu), whole(w_ku), whole(w_vu)],
        out_specs=[rows(3 * SB_WIDTH), rows(width), rows(width), rows(width), rows(2 * d)],
        out_shape=[jax.ShapeDtypeStruct((n, 3 * SB_WIDTH), BF16),
                   jax.ShapeDtypeStruct((n, width), BF16),
                   jax.ShapeDtypeStruct((n, width), BF16),
                   jax.ShapeDtypeStruct((n, width), BF16),
                   jax.ShapeDtypeStruct((n, 2 * d), BF16)],
        compiler_params=_cparams(("arbitrary",)),
        name="in_proj",
    )(x2, mod4, mod4, posf, invf, w_sb, w_dn, w_kr2, w_g, qg, kvg, w_qu, w_ku, w_vu)


def _item_schedule(n_q, q_per_k):
    q_of, t_of = [], []
    for qi in range(n_q):
        for t in range(qi // q_per_k + 1):
            q_of.append(qi)
            t_of.append(t)
    return np.asarray(q_of, np.int32), np.asarray(t_of, np.int32)


def _causal_bias(tq, tk, strict):
    c = jnp.arange(tk)[None, :]
    tiles = [jnp.zeros((tq, tk), F32)]
    for r in range(tk // tq):
        q_pos = jnp.arange(tq)[:, None] + r * tq
        visible = (c < q_pos) if strict else (c <= q_pos)
        tiles.append(jnp.where(visible, 0.0, MASK_BIAS).astype(F32))
    return jnp.stack(tiles)


def _pipeline3(n, stage_a, stage_b, stage_c, newest_first):
    stage_a(0, 0)
    if n >= 2:
        stage_a(1, 1)
    stage_b(0, 0)

    def step(i, slot):
        stages = [lambda: stage_a(i, slot), lambda: stage_b(i - 1, 1 - slot), lambda: stage_c(i - 2, slot)]
        for run in (stages if newest_first else reversed(stages)):
            run()

    def body(k, carry):
        for j in range(PIPELINE_UNROLL):
            step(2 + PIPELINE_UNROLL * k + j, j % 2)
        return carry

    bodies = max(n - 2, 0) // PIPELINE_UNROLL
    lax.fori_loop(0, bodies, body, 0)
    for i in range(2 + bodies * PIPELINE_UNROLL, n):
        step(i, i % 2)
    if n >= 2:
        stage_c(n - 2, n % 2)
        stage_b(n - 1, (n - 1) % 2)
    stage_c(n - 1, (n - 1) % 2)


def _suffix_sum_matrix():
    j = jnp.arange(2 * LANES)[:, None] % LANES
    n = jnp.arange(2 * LANES)[None, :]
    return jnp.where(n < LANES, (j >= n), True).astype(F32)


def _sb_kernel(item_q_ref, item_t_ref, q_ref, k_ref, v_ref, tri_ref, bias_ref, o_ref,
               qh_ref, c_ref, acc_ref, z_ref, w_ref, *, tq, tk, n_pairs, n_items):
    lane = lax.broadcasted_iota(jnp.int32, (tq, LANES), 1)
    n_sub = tk // LANES
    n_chains = 2 * n_pairs

    for p in range(n_pairs):
        qp = q_ref[:, p * LANES:(p + 1) * LANES]
        q_lane = lax.broadcasted_iota(jnp.int32, qp.shape, 1)
        for h in range(2):
            head_lanes = (q_lane < SB_HEAD_DIM) if h == 0 else (q_lane >= SB_HEAD_DIM)
            qh_ref[2 * p + h] = jnp.where(head_lanes, qp, jnp.zeros_like(qp))
    c_ref[...] = jnp.zeros_like(c_ref)
    acc_ref[...] = jnp.zeros_like(acc_ref)

    def item(f):
        qi = item_q_ref[f]
        t = item_t_ref[f]
        q_rows = pl.ds(pl.multiple_of(qi * tq, tq), tq)
        k_rows = pl.ds(pl.multiple_of((qi - t) * tk, tk), tk)
        return q_rows, k_rows, t

    def pair_lanes(ci):
        return slice((ci // 2) * LANES, (ci // 2 + 1) * LANES)

    def scores(f, slot):
        q_rows, k_rows, t = item(f)
        bias = bias_ref[(t == 0).astype(jnp.int32)]
        for ci in range(n_chains):
            z_ref[slot, ci] = _dot_t(qh_ref[ci, q_rows, :], k_ref[k_rows, pair_lanes(ci)]) + bias

    def weights(f, slot):
        _, _, t = item(f)
        keep = jnp.where(t == 0, 0.0, 1.0)
        tri = tri_ref[...]
        sign = jnp.uint32(0x80000000)
        bf16_bits = jnp.uint32(0xFFFF0000)
        for ci in range(n_chains):
            c = c_ref[ci] * keep
            for s in reversed(range(n_sub)):
                z = z_ref[slot, ci, :, s * LANES:(s + 1) * LANES]
                neg_abs = pltpu.bitcast(pltpu.bitcast(z, jnp.uint32) | sign, F32)
                sp = jnp.maximum(z, 0.0) + jnp.log2(1.0 + jnp.exp2(neg_abs))
                hi = pltpu.bitcast(pltpu.bitcast(sp, jnp.uint32) & bf16_bits, F32)
                r = _dot(jnp.concatenate([hi, sp - hi], axis=1), tri)
                w_ref[slot, ci, :, s * LANES:(s + 1) * LANES] = jnp.exp2(z - (c + r[:, :LANES])).astype(BF16)
                c = c + r[:, LANES:]
            c_ref[ci] = c

    def values(f, slot):
        q_rows, k_rows, t = item(f)
        keep = jnp.where(t == 0, 0.0, 1.0)
        for p in range(n_pairs):
            accs = []
            for ci in (2 * p, 2 * p + 1):
                acc = acc_ref[ci] * keep + _dot(w_ref[slot, ci], v_ref[k_rows, pair_lanes(ci)])
                acc_ref[ci] = acc
                accs.append(acc)
            o_ref[q_rows, p * LANES:(p + 1) * LANES] = jnp.where(lane < SB_HEAD_DIM, accs[0], accs[1]).astype(o_ref.dtype)

    _pipeline3(n_items, scores, weights, values, newest_first=True)


def _sb_attention(sb3, tri):
    bsz, seq, _ = sb3.shape
    tq = min(ATTN_Q, seq)
    tk = tq
    n_pairs = SB_PAIRS_PER_STEP
    width = n_pairs * LANES
    groups = SB_WIDTH // width
    item_q, item_t = _item_schedule(seq // tq, 1)
    bias = _causal_bias(tq, tk, strict=True)

    def seq_block(first_block):
        return pl.BlockSpec((None, seq, width), lambda b, g, iq, it: (b, 0, first_block + g))

    grid_spec = pltpu.PrefetchScalarGridSpec(
        num_scalar_prefetch=2,
        grid=(bsz, groups),
        in_specs=[seq_block(0), seq_block(groups), seq_block(2 * groups),
                  pl.BlockSpec(tri.shape, lambda b, g, iq, it: (0, 0)),
                  pl.BlockSpec(bias.shape, lambda b, g, iq, it: (0, 0, 0))],
        out_specs=seq_block(0),
        scratch_shapes=[pltpu.VMEM((2 * n_pairs, seq, LANES), BF16),
                        pltpu.VMEM((2 * n_pairs, tq, LANES), F32),
                        pltpu.VMEM((2 * n_pairs, tq, LANES), F32),
                        pltpu.VMEM((2, 2 * n_pairs, tq, tk), F32),
                        pltpu.VMEM((2, 2 * n_pairs, tq, tk), BF16)],
    )
    return pl.pallas_call(
        functools.partial(_sb_kernel, tq=tq, tk=tk, n_pairs=n_pairs, n_items=len(item_q)),
        grid_spec=grid_spec,
        out_shape=jax.ShapeDtypeStruct((bsz, seq, SB_WIDTH), BF16),
        compiler_params=_cparams(("arbitrary", "arbitrary")),
        name="sb_attn",
    )(item_q, item_t, sb3, sb3, sb3, tri, bias)


def _mla_kernel(item_q_ref, item_t_ref, q_ref, k_ref, v_ref, bias_ref, o_ref,
                m_ref, s_ref, p_ref, alpha_ref, *, tq, tk, n_heads, n_items):
    n_sub = tk // LANES
    m_ref[...] = jnp.zeros_like(m_ref)
    o_ref[...] = jnp.zeros_like(o_ref)

    q_per_k = tk // tq

    def item(f):
        qi = item_q_ref[f]
        t = item_t_ref[f]
        q_rows = pl.ds(pl.multiple_of(qi * tq, tq), tq)
        k_rows = pl.ds(pl.multiple_of((qi // q_per_k - t) * tk, tk), tk)
        return q_rows, k_rows, t, qi

    def head_lanes(ci):
        return slice(ci * MLA_HEAD_LANES, (ci + 1) * MLA_HEAD_LANES)

    def scores(f, slot):
        q_rows, k_rows, t, qi = item(f)
        bias = bias_ref[jnp.where(t == 0, 1 + qi % q_per_k, 0)]
        for ci in range(n_heads):
            s_ref[slot, ci] = _dot_t(q_ref[q_rows, head_lanes(ci)], k_ref[k_rows, head_lanes(ci)]) + bias

    def numerators(f, slot):
        _, _, t, _ = item(f)
        for ci in range(n_heads):
            row_max = jnp.max(s_ref[slot, ci], axis=-1, keepdims=True)
            m = jnp.where(t == 0, -jnp.inf, m_ref[ci])
            m_new = jnp.maximum(m, jnp.broadcast_to(row_max, (tq, LANES)))
            alpha_ref[slot, ci] = jnp.exp2(m - m_new)
            for j in range(n_sub):
                sl = slice(j * LANES, (j + 1) * LANES)
                p_ref[slot, ci, :, sl] = jnp.exp2(s_ref[slot, ci, :, sl] - m_new).astype(BF16)
            m_ref[ci] = m_new

    def values(f, slot):
        q_rows, k_rows, _, _ = item(f)
        for ci in range(n_heads):
            acc = o_ref[q_rows, head_lanes(ci)]
            o_ref[q_rows, head_lanes(ci)] = (alpha_ref[slot, ci] * acc
                                             + _dot(p_ref[slot, ci], v_ref[k_rows, head_lanes(ci)]))

    _pipeline3(n_items, scores, numerators, values, newest_first=False)


def _mla_attention(qm3, km3, vm3):
    bsz, seq, _ = qm3.shape
    tq = min(ATTN_Q, seq)
    tk = min(MLA_ATTN_K, seq)
    n_heads = MLA_HEADS_PER_STEP
    groups = MLA_HEADS // n_heads
    width = n_heads * MLA_HEAD_LANES
    item_q, item_t = _item_schedule(seq // tq, tk // tq)
    bias = _causal_bias(tq, tk, strict=False)
    seq_block = pl.BlockSpec((None, seq, width), lambda b, g, iq, it: (b, 0, g))
    grid_spec = pltpu.PrefetchScalarGridSpec(
        num_scalar_prefetch=2,
        grid=(bsz, groups),
        in_specs=[seq_block, seq_block, seq_block, pl.BlockSpec(bias.shape, lambda b, g, iq, it: (0, 0, 0))],
        out_specs=seq_block,
        scratch_shapes=[pltpu.VMEM((n_heads, tq, LANES), F32),
                        pltpu.VMEM((2, n_heads, tq, tk), F32),
                        pltpu.VMEM((2, n_heads, tq, tk), BF16),
                        pltpu.VMEM((2, n_heads, tq, LANES), F32)],
    )
    return pl.pallas_call(
        functools.partial(_mla_kernel, tq=tq, tk=tk, n_heads=n_heads, n_items=len(item_q)),
        grid_spec=grid_spec,
        out_shape=jax.ShapeDtypeStruct((bsz, seq, MLA_HEADS * MLA_HEAD_LANES), F32),
        compiler_params=_cparams(("arbitrary", "arbitrary")),
        name="mla_attn",
    )(item_q, item_t, qm3, km3, vm3, bias)


def _layer_norm(v, g, b):
    mu = jnp.mean(v, axis=-1, keepdims=True)
    dv = v - mu
    var = jnp.mean(dv * dv, axis=-1, keepdims=True)
    return dv * lax.rsqrt(var + LN_EPS) * g + b


ROUTER_NO_LANE = 1 << 20


def _router_logits(u2, wr_ref, br_ref):
    return _dot(u2.astype(BF16), wr_ref[...]) + br_ref[...]


def _group_lanes(lane):
    return (lane >= N_EXPERTS) & (lane < N_EXPERTS + N_GROUPS)


def _merge_kernel(osb_ref, omla_ref, g_ref, x_ref, gate1_ref, scale2_ref, shift2_ref,
                  wsb_ref, wmla_ref, wout_ref, ln_g_ref, ln_b_ref, wr_ref, br_ref,
                  x1_ref, u2_ref, gidx_ref):
    d = x_ref.shape[-1]
    y_sb = _dot(osb_ref[...], wsb_ref[...])
    lane = lax.broadcasted_iota(jnp.int32, (osb_ref.shape[0], LANES), 1)
    o_mla = []
    for p in range(MLA_HEADS // 2):
        res = []
        for h in range(2):
            acc = omla_ref[:, (2 * p + h) * MLA_HEAD_LANES:(2 * p + h + 1) * MLA_HEAD_LANES]
            res.append(acc / pltpu.roll(acc, MLA_V_DIM, 1))
        o_mla.append(jnp.where(lane < MLA_V_DIM, res[0], pltpu.roll(res[1], MLA_V_DIM, 1)).astype(BF16))
    y_mla = _dot(jnp.concatenate(o_mla, axis=1), wmla_ref[...])
    gates = g_ref[...].astype(F32)
    mixed = jax.nn.sigmoid(gates[:, :d]) * y_sb + jax.nn.sigmoid(gates[:, d:]) * y_mla
    attn = _dot(mixed.astype(BF16), wout_ref[...])
    x1 = _layer_norm(DEEPNORM_ALPHA * x_ref[...] + gate1_ref[...] * attn, ln_g_ref[...], ln_b_ref[...])
    x1_ref[...] = x1
    u2 = x1 * (1.0 + scale2_ref[...]) + shift2_ref[...]
    u2_ref[...] = u2
    logits = _router_logits(u2, wr_ref, br_ref)
    lane = lax.broadcasted_iota(jnp.int32, logits.shape, 1)
    g_log = jnp.where(_group_lanes(lane), logits, -jnp.inf)
    g_max = jnp.max(g_log, axis=-1, keepdims=True)
    gidx_ref[...] = jnp.min(jnp.where(g_log == g_max, lane, ROUTER_NO_LANE), axis=-1, keepdims=True) - N_EXPERTS


def _merge(o_sb, o_mla, gates, x2, mod4, w_bsb, w_bmla, w_out, ln_g, ln_b, w_r, b_r, seq):
    n, d = x2.shape
    tm = min(MERGE_ROWS, seq)
    tiles_per_seq = seq // tm

    def whole(a):
        return pl.BlockSpec(a.shape, lambda i: (0,) * a.ndim)

    def mod_spec(k):
        return pl.BlockSpec((None, None, 1, d), lambda i: (i // tiles_per_seq, k, 0, 0))

    def rows(width):
        return pl.BlockSpec((tm, width), lambda i: (i, 0))

    return pl.pallas_call(
        _merge_kernel,
        grid=(n // tm,),
        in_specs=[rows(SB_WIDTH), rows(MLA_HEADS * MLA_HEAD_LANES), rows(2 * d), rows(d),
                  mod_spec(2), mod_spec(4), mod_spec(3),
                  whole(w_bsb), whole(w_bmla), whole(w_out), whole(ln_g), whole(ln_b), whole(w_r), whole(b_r)],
        out_specs=[rows(d), rows(d), rows(1)],
        out_shape=[jax.ShapeDtypeStruct((n, d), F32),
                   jax.ShapeDtypeStruct((n, d), F32),
                   jax.ShapeDtypeStruct((n, 1), jnp.int32)],
        compiler_params=_cparams(("arbitrary",)),
        name="merge",
    )(o_sb, o_mla, gates, x2, mod4, mod4, mod4, w_bsb, w_bmla, w_out, ln_g, ln_b, w_r, b_r)


def _route_plan(gidx, tm):
    n = gidx.shape[0]
    max_tiles = n // tm + N_GROUPS
    order = jnp.argsort(gidx, stable=True).astype(jnp.int32)
    counts = jnp.sum(gidx[:, None] == jnp.arange(N_GROUPS, dtype=jnp.int32)[None, :], axis=0).astype(jnp.int32)
    tiles = (counts + tm - 1) // tm
    tile_end = jnp.cumsum(tiles)
    tile_start = tile_end - tiles
    run_start = jnp.cumsum(counts) - counts
    tile_ids = jnp.arange(max_tiles, dtype=jnp.int32)
    n_tiles = tile_end[-1]
    tile_group = jnp.minimum(jnp.sum(tile_ids[:, None] >= tile_end[None, :], axis=1), N_GROUPS - 1).astype(jnp.int32)
    rows_before = (tile_ids - tile_start[tile_group]) * tm
    tile_valid = jnp.clip(counts[tile_group] - rows_before, 0, tm)
    tile_valid = jnp.where(tile_ids < n_tiles, tile_valid, 0).astype(jnp.int32)
    row = jnp.arange(tm, dtype=jnp.int32)[None, :]
    sorted_pos = (run_start[tile_group] + rows_before)[:, None] + row
    slot_token = jnp.where(row < tile_valid[:, None], order[jnp.clip(sorted_pos, 0, n - 1)], 0)
    return tile_group, tile_valid, n_tiles.reshape(1).astype(jnp.int32), slot_token.reshape(max_tiles, 1, tm)


def _experts_kernel(tile_group_ref, tile_valid_ref, n_tiles_ref, tok_ref, tok_next_ref,
                    u2_hbm, wr_ref, br_ref, wg_ref, wu_ref, wd_ref, y_hbm,
                    xbuf, ybuf, gather_sem, scatter_sem, *, tm):
    i = pl.program_id(0)
    n_tiles = n_tiles_ref[0]
    buf = i % 2

    def gather_row(tok, r, b):
        return pltpu.make_async_copy(u2_hbm.at[pl.ds(tok, 1)], xbuf.at[b, pl.ds(r, 1)], gather_sem.at[b])

    def scatter_row(tok, r):
        return pltpu.make_async_copy(ybuf.at[pl.ds(r, 1)], y_hbm.at[pl.ds(tok, 1)], scatter_sem)

    def start_gather(idx_ref, b):
        def body(r, carry):
            gather_row(idx_ref[0, r], r, b).start()
            return carry
        lax.fori_loop(0, tm, body, 0, unroll=DMA_ISSUE_UNROLL)

    def start_scatter(n_rows):
        def body(r, carry):
            scatter_row(tok_ref[0, r], r).start()
            return carry

        @pl.when(n_rows == tm)
        def _():
            lax.fori_loop(0, tm, body, 0, unroll=DMA_ISSUE_UNROLL)

        @pl.when(n_rows != tm)
        def _():
            lax.fori_loop(0, n_rows, body, 0)

    def wait_scatter(n_rows):
        def body(r, carry):
            scatter_row(0, r).wait()
            return carry

        @pl.when(n_rows == tm)
        def _():
            pltpu.make_async_copy(ybuf, ybuf, scatter_sem).wait()

        @pl.when(n_rows != tm)
        def _():
            lax.fori_loop(0, n_rows, body, 0)

    @pl.when(i == 0)
    def _():
        start_gather(tok_ref, 0)

    @pl.when(i < n_tiles)
    def _():
        pltpu.make_async_copy(xbuf.at[buf], xbuf.at[buf], gather_sem.at[buf]).wait()

        @pl.when(i + 1 < n_tiles)
        def _():
            start_gather(tok_next_ref, 1 - buf)

        group = tile_group_ref[i]
        x = xbuf[buf]
        logits = _router_logits(x, wr_ref, br_ref)
        lane = lax.broadcasted_iota(jnp.int32, logits.shape, 1)
        neg = -jnp.inf
        g_log = jnp.where(_group_lanes(lane), logits, neg)
        g_max = jnp.max(g_log, axis=-1, keepdims=True)
        p_group = 1.0 / jnp.sum(jnp.exp(g_log - g_max), axis=-1, keepdims=True)
        first = group * EXPERTS_PER_GROUP
        e_log = jnp.where((lane >= first) & (lane < first + EXPERTS_PER_GROUP), logits, neg)
        v1 = jnp.max(e_log, axis=-1, keepdims=True)
        i1 = jnp.min(jnp.where(e_log == v1, lane, ROUTER_NO_LANE), axis=-1, keepdims=True)
        e_log2 = jnp.where(lane == i1, neg, e_log)
        v2 = jnp.max(e_log2, axis=-1, keepdims=True)
        i2 = jnp.min(jnp.where(e_log2 == v2, lane, ROUTER_NO_LANE), axis=-1, keepdims=True)
        e2 = jnp.exp(v2 - v1)
        w1 = p_group / (1.0 + e2)
        w2 = p_group * e2 / (1.0 + e2)
        combine = jnp.where(lane == i1, w1, 0.0) + jnp.where(lane == i2, w2, 0.0)

        xb = x.astype(BF16)
        hidden = []
        for e in range(EXPERTS_PER_GROUP):
            w_tok = jnp.sum(jnp.where(lane == first + e, combine, 0.0), axis=-1, keepdims=True)
            hg = _dot(xb, wg_ref[e])
            hu = _dot(xb, wu_ref[e])
            hidden.append((hg * jax.nn.sigmoid(hg) * hu * w_tok).astype(BF16))
        y = _dot(jnp.concatenate(hidden, axis=1), wd_ref[...])

        @pl.when(i >= 1)
        def _():
            wait_scatter(tile_valid_ref[jnp.maximum(i - 1, 0)])

        ybuf[...] = y
        n_valid = tile_valid_ref[i]
        start_scatter(n_valid)

        @pl.when(i == n_tiles - 1)
        def _():
            wait_scatter(n_valid)


def _experts(u2, gidx, w_r, b_r, w_g, w_u, w_d):
    n, d = u2.shape
    tm = MOE_ROWS
    d_e = w_g.shape[-1]
    tile_group, tile_valid, n_tiles, slot_token = _route_plan(gidx, tm)
    max_tiles = slot_token.shape[0]
    w_g = w_g.reshape(N_GROUPS, EXPERTS_PER_GROUP, d, d_e)
    w_u = w_u.reshape(N_GROUPS, EXPERTS_PER_GROUP, d, d_e)
    w_d = w_d.reshape(N_GROUPS, EXPERTS_PER_GROUP * d_e, d)
    grid_spec = pltpu.PrefetchScalarGridSpec(
        num_scalar_prefetch=3,
        grid=(max_tiles,),
        in_specs=[pl.BlockSpec((None, 1, tm), lambda i, tg, tv, nt: (i, 0, 0), memory_space=pltpu.SMEM),
                  pl.BlockSpec((None, 1, tm), lambda i, tg, tv, nt: (jnp.minimum(i + 1, max_tiles - 1), 0, 0),
                               memory_space=pltpu.SMEM),
                  pl.BlockSpec(memory_space=pl.ANY),
                  pl.BlockSpec(w_r.shape, lambda i, tg, tv, nt: (0, 0)),
                  pl.BlockSpec(b_r.shape, lambda i, tg, tv, nt: (0, 0)),
                  pl.BlockSpec((None, EXPERTS_PER_GROUP, d, d_e), lambda i, tg, tv, nt: (tg[i], 0, 0, 0)),
                  pl.BlockSpec((None, EXPERTS_PER_GROUP, d, d_e), lambda i, tg, tv, nt: (tg[i], 0, 0, 0)),
                  pl.BlockSpec((None, EXPERTS_PER_GROUP * d_e, d), lambda i, tg, tv, nt: (tg[i], 0, 0))],
        out_specs=pl.BlockSpec(memory_space=pl.ANY),
        scratch_shapes=[pltpu.VMEM((2, tm, d), F32),
                        pltpu.VMEM((tm, d), F32),
                        pltpu.SemaphoreType.DMA((2,)),
                        pltpu.SemaphoreType.DMA(())],
    )
    return pl.pallas_call(
        functools.partial(_experts_kernel, tm=tm),
        grid_spec=grid_spec,
        out_shape=jax.ShapeDtypeStruct((n, d), F32),
        compiler_params=_cparams(("arbitrary",)),
        name="experts",
    )(tile_group, tile_valid, n_tiles, slot_token, slot_token, u2, w_r, b_r, w_g, w_u, w_d)


def _ln2_kernel(x1_ref, y_ref, gate2_ref, ln_g_ref, ln_b_ref, o_ref):
    v = DEEPNORM_ALPHA * x1_ref[...] + gate2_ref[...] * y_ref[...]
    o_ref[...] = _layer_norm(v, ln_g_ref[...], ln_b_ref[...])


def _ln2(x1, y, mod4, ln_g, ln_b, seq):
    n, d = x1.shape
    tm = min(LN2_ROWS, seq)
    tiles_per_seq = seq // tm
    rows = pl.BlockSpec((tm, d), lambda i: (i, 0))
    vec = pl.BlockSpec((1, d), lambda i: (0, 0))
    return pl.pallas_call(
        _ln2_kernel,
        grid=(n // tm,),
        in_specs=[rows, rows, pl.BlockSpec((None, None, 1, d), lambda i: (i // tiles_per_seq, 5, 0, 0)), vec, vec],
        out_specs=rows,
        out_shape=jax.ShapeDtypeStruct((n, d), F32),
        compiler_params=_cparams(("arbitrary",)),
        name="ln2",
    )(x1, y, mod4, ln_g, ln_b)


def kernel(x, c, positions, w_ada, b_ada, w_in, mla_q_norm_g, w_q_up, mla_kv_norm_g, w_kv_up, w_branch_sb, w_branch_mla, w_out, ln1_g, ln1_b, w_router_group, b_router_group, w_router_expert, b_router_expert, w_exp_gate, w_exp_up, w_exp_down, ln2_g, ln2_b):
    bsz, seq, d = x.shape
    n = bsz * seq
    invf = _rope_inv_freq_tile()
    posf = positions.astype(F32).reshape(n, 1)
    tri = _suffix_sum_matrix()
    for l in range(w_ada.shape[0]):
        x2 = x.reshape(n, d)
        mod4 = _adaln_mod(c, w_ada[l], b_ada[l]).reshape(bsz, N_MOD, 1, d)
        weights = _prep_in_weights(w_in[l], w_q_up[l], w_kv_up[l])
        sb, qm, km, vm, gates = _in_proj(x2, mod4, posf, invf, weights,
                                         mla_q_norm_g[l].reshape(1, -1), mla_kv_norm_g[l].reshape(1, -1), seq)
        o_sb = _sb_attention(sb.reshape(bsz, seq, -1), tri).reshape(n, SB_WIDTH)
        o_mla = _mla_attention(qm.reshape(bsz, seq, -1), km.reshape(bsz, seq, -1),
                               vm.reshape(bsz, seq, -1)).reshape(n, -1)
        w_r = jnp.zeros((d, LANES), F32)
        w_r = w_r.at[:, :N_EXPERTS].set(w_router_expert[l]).at[:, N_EXPERTS:N_EXPERTS + N_GROUPS].set(w_router_group[l])
        b_r = jnp.zeros((1, LANES), F32)
        b_r = b_r.at[0, :N_EXPERTS].set(b_router_expert[l]).at[0, N_EXPERTS:N_EXPERTS + N_GROUPS].set(b_router_group[l])
        w_r = w_r.astype(BF16)
        x1, u2, gidx = _merge(o_sb, o_mla, gates, x2, mod4,
                              w_branch_sb[l].astype(BF16), w_branch_mla[l].astype(BF16), w_out[l].astype(BF16),
                              ln1_g[l].reshape(1, d), ln1_b[l].reshape(1, d), w_r, b_r, seq)
        y = _experts(u2, gidx.reshape(n), w_r, b_r, w_exp_gate[l].astype(BF16), w_exp_up[l].astype(BF16),
                     w_exp_down[l].astype(BF16))
        out = _ln2(x1, y, mod4, ln2_g[l].reshape(1, d), ln2_b[l].reshape(1, d), seq)
        x = out.reshape(bsz, seq, d)
    return x
```

```python
import functools
import math

import jax
import jax.numpy as jnp
import numpy as np
from jax import lax
from jax.experimental import pallas as pl
from jax.experimental.pallas import tpu as pltpu

F32 = jnp.float32
BF16 = jnp.bfloat16

D_MODEL = 1024
SB_HEADS = 8
SB_HEAD_DIM = 64
SB_WIDTH = SB_HEADS * SB_HEAD_DIM
MLA_HEADS = 8
MLA_NOPE_DIM = 64
MLA_ROPE_DIM = 32
MLA_V_DIM = 64
MLA_Q_RANK = 384
MLA_KV_RANK = 256
MLA_QK_DIM = MLA_NOPE_DIM + MLA_ROPE_DIM
MLA_WIDTH = MLA_HEADS * MLA_V_DIM
ROPE_THETA = 10000.0
N_GROUPS = 4
EXPERTS_PER_GROUP = 8
N_EXPERTS = N_GROUPS * EXPERTS_PER_GROUP
D_EXPERT = 256
DEPTH = 1
DEEPNORM_ALPHA = (2.0 * DEPTH) ** 0.25
LN_EPS = 1e-5
RMS_EPS = 1e-6
LOG2_E = math.log2(math.e)
MASK_BIAS = -1e30
N_MOD = 6

LANES = 128
MLA_HEAD_LANES = 128
VMEM_LIMIT_BYTES = 56 * 1024 * 1024

PROJ_ROWS = 512
ATTN_Q = 256
MLA_ATTN_K = 512
MERGE_ROWS = 512
MOE_ROWS = 512
LN2_ROWS = 512
DMA_ISSUE_UNROLL = 8
SB_PAIRS_PER_STEP = 2
MLA_HEADS_PER_STEP = 4
PIPELINE_UNROLL = 4


def _cparams(sem):
    return pltpu.CompilerParams(dimension_semantics=sem, vmem_limit_bytes=VMEM_LIMIT_BYTES)


def _dot(a, b):
    return jnp.dot(a, b, preferred_element_type=F32)


def _dot_t(a, b):
    return lax.dot_general(a, b, (((1,), (1,)), ((), ())), preferred_element_type=F32)


def _adaln_kernel(c_ref, w_ref, b_ref, o_ref):
    c = c_ref[...]
    ca = c * jax.nn.sigmoid(c)
    o_ref[...] = _dot(ca.astype(BF16), w_ref[...].astype(BF16)) + b_ref[...]


def _adaln_mod(c, w_ada, b_ada):
    bsz, d = c.shape
    return pl.pallas_call(
        _adaln_kernel,
        grid=(N_MOD,),
        in_specs=[pl.BlockSpec((bsz, d), lambda j: (0, 0)),
                  pl.BlockSpec((d, d), lambda j: (0, j)),
                  pl.BlockSpec((1, d), lambda j: (0, j))],
        out_specs=pl.BlockSpec((bsz, d), lambda j: (0, j)),
        out_shape=jax.ShapeDtypeStruct((bsz, N_MOD * d), F32),
        compiler_params=_cparams(("arbitrary",)),
        name="adaln_mod",
    )(c, w_ada, b_ada.reshape(1, N_MOD * d))


def _in_proj_kernel(x_ref, scale_ref, shift_ref, pos_ref, invf_ref,
                    w_sb_ref, w_dn_ref, w_kr_ref, w_g_ref, qg_ref, kvg_ref,
                    w_qu_ref, w_kvu_ref,
                    sb_ref, qm_ref, km_ref, vm_ref, g_ref):
    u = (x_ref[...] * (1.0 + scale_ref[...]) + shift_ref[...]).astype(BF16)

    sb = _dot(u, w_sb_ref[...])
    sb_scale = LOG2_E / math.sqrt(SB_HEAD_DIM)
    sb_ref[:, :SB_WIDTH] = (sb[:, :SB_WIDTH] * sb_scale).astype(BF16)
    sb_ref[:, SB_WIDTH:] = sb[:, SB_WIDTH:].astype(BF16)

    g_ref[...] = _dot(u, w_g_ref[...]).astype(BF16)

    ang = pos_ref[...] * invf_ref[...]
    cos = jnp.cos(ang)
    sin = jnp.sin(ang)

    dn = _dot(u, w_dn_ref[...])
    q_dn = dn[:, :MLA_Q_RANK]
    kv_dn = dn[:, MLA_Q_RANK:]
    qn = q_dn * lax.rsqrt(jnp.mean(q_dn * q_dn, axis=-1, keepdims=True) + RMS_EPS) * qg_ref[...]
    kvn = kv_dn * lax.rsqrt(jnp.mean(kv_dn * kv_dn, axis=-1, keepdims=True) + RMS_EPS) * kvg_ref[...]
    qn = qn.astype(BF16)
    kvn = kvn.astype(BF16)

    width = MLA_HEADS * MLA_HEAD_LANES
    qu = _dot(qn, w_qu_ref[...])
    kvu = _dot(kvn, w_kvu_ref[...])
    kr = _dot(u, w_kr_ref[...])
    kr = kr[:, :LANES] * cos + kr[:, LANES:] * sin
    lane = lax.broadcasted_iota(jnp.int32, (u.shape[0], LANES), 1)
    mla_scale = LOG2_E / math.sqrt(MLA_QK_DIM)
    heads_per_rot_tile = LANES // MLA_ROPE_DIM
    for h in range(MLA_HEADS):
        sl = slice(h * MLA_HEAD_LANES, (h + 1) * MLA_HEAD_LANES)
        rot_tile = qu[:, width + (h // heads_per_rot_tile) * LANES:width + (h // heads_per_rot_tile + 1) * LANES]
        shift = (MLA_NOPE_DIM - (h % heads_per_rot_tile) * MLA_ROPE_DIM) % LANES
        rot = pltpu.roll(rot_tile, shift, 1) if shift else rot_tile
        qm_ref[:, sl] = ((qu[:, sl] * cos + rot * sin) * mla_scale).astype(BF16)
        kv_h = kvu[:, sl]
        km_ref[:, sl] = (jnp.where(lane < MLA_NOPE_DIM, kv_h, 0.0) + kr).astype(BF16)
        vm_ref[:, sl] = jnp.where(lane < MLA_V_DIM, pltpu.roll(kv_h, MLA_V_DIM, 1), 1.0).astype(BF16)


def _rope_inv_freq_tile():
    inv_freq = 1.0 / (ROPE_THETA ** (jnp.arange(0, MLA_ROPE_DIM, 2, dtype=F32) / MLA_ROPE_DIM))
    tile = jnp.zeros((LANES,), F32)
    half = MLA_ROPE_DIM // 2
    tile = tile.at[MLA_NOPE_DIM:MLA_NOPE_DIM + half].set(inv_freq)
    tile = tile.at[MLA_NOPE_DIM + half:MLA_NOPE_DIM + 2 * half].set(inv_freq)
    return tile.reshape(1, LANES)


def _rotate_half_cols(w):
    half = MLA_ROPE_DIM // 2
    return jnp.concatenate([-w[..., half:], w[..., :half]], axis=-1)


def _prep_in_weights(w_in, w_q_up, w_kv_up):
    o = 0
    w_sb = w_in[:, o:o + 3 * SB_WIDTH]; o += 3 * SB_WIDTH
    w_dn = w_in[:, o:o + MLA_Q_RANK + MLA_KV_RANK]; o += MLA_Q_RANK + MLA_KV_RANK
    w_kr = w_in[:, o:o + MLA_ROPE_DIM]; o += MLA_ROPE_DIM
    w_g = w_in[:, o:]

    def rope_tile(w):
        z = jnp.zeros((w.shape[0], LANES), w.dtype)
        return z.at[:, MLA_NOPE_DIM:MLA_NOPE_DIM + MLA_ROPE_DIM].set(w)

    w_kr2 = jnp.concatenate([rope_tile(w_kr), rope_tile(_rotate_half_cols(w_kr))], axis=1)

    wq = w_q_up.reshape(MLA_Q_RANK, MLA_HEADS, MLA_QK_DIM)
    q_base = jnp.zeros((MLA_Q_RANK, MLA_HEADS, MLA_HEAD_LANES), w_q_up.dtype).at[:, :, :MLA_QK_DIM].set(wq)
    q_rot = _rotate_half_cols(wq[:, :, MLA_NOPE_DIM:])
    w_qu = jnp.concatenate([q_base.reshape(MLA_Q_RANK, -1), q_rot.reshape(MLA_Q_RANK, -1)], axis=1)
    return tuple(w.astype(BF16) for w in (w_sb, w_dn, w_kr2, w_g, w_qu, w_kv_up))


def _in_proj(x2, mod4, posf, invf, weights, qg, kvg, seq):
    n, d = x2.shape
    tm = min(PROJ_ROWS, seq)
    tiles_per_seq = seq // tm
    w_sb, w_dn, w_kr2, w_g, w_qu, w_kvu = weights

    def whole(a):
        return pl.BlockSpec(a.shape, lambda i: (0,) * a.ndim)

    def mod_spec(k):
        return pl.BlockSpec((None, None, 1, d), lambda i: (i // tiles_per_seq, k, 0, 0))

    def rows(width):
        return pl.BlockSpec((tm, width), lambda i: (i, 0))

    width = MLA_HEADS * MLA_HEAD_LANES
    return pl.pallas_call(
        _in_proj_kernel,
        grid=(n // tm,),
        in_specs=[rows(d), mod_spec(1), mod_spec(0), rows(1), whole(invf),
                  whole(w_sb), whole(w_dn), whole(w_kr2), whole(w_g), whole(qg), whole(kvg),
                  whole(w_qu), whole(w_kvu)],
        out_specs=[rows(3 * SB_WIDTH), rows(width), rows(width), rows(width), rows(2 * d)],
        out_shape=[jax.ShapeDtypeStruct((n, 3 * SB_WIDTH), BF16),
                   jax.ShapeDtypeStruct((n, width), BF16),
                   jax.ShapeDtypeStruct((n, width), BF16),
                   jax.ShapeDtypeStruct((n, width), BF16),
                   jax.ShapeDtypeStruct((n, 2 * d), BF16)],
        compiler_params=_cparams(("arbitrary",)),
        name="in_proj",
    )(x2, mod4, mod4, posf, invf, w_sb, w_dn, w_kr2, w_g, qg, kvg, w_qu, w_kvu)


def _item_schedule(n_q, q_per_k):
    q_of, t_of = [], []
    for qi in range(n_q):
        for t in range(qi // q_per_k + 1):
            q_of.append(qi)
            t_of.append(t)
    return np.asarray(q_of, np.int32), np.asarray(t_of, np.int32)


def _causal_bias(tq, tk, strict):
    c = jnp.arange(tk)[None, :]
    tiles = [jnp.zeros((tq, tk), F32)]
    for r in range(tk // tq):
        q_pos = jnp.arange(tq)[:, None] + r * tq
        visible = (c < q_pos) if strict else (c <= q_pos)
        tiles.append(jnp.where(visible, 0.0, MASK_BIAS).astype(F32))
    return jnp.stack(tiles)


def _pipeline3(n, stage_a, stage_b, stage_c, newest_first):
    stage_a(0, 0)
    if n >= 2:
        stage_a(1, 1)
    stage_b(0, 0)

    def step(i, slot):
        stages = [lambda: stage_a(i, slot), lambda: stage_b(i - 1, 1 - slot), lambda: stage_c(i - 2, slot)]
        for run in (stages if newest_first else reversed(stages)):
            run()

    def body(k, carry):
        for j in range(PIPELINE_UNROLL):
            step(2 + PIPELINE_UNROLL * k + j, j % 2)
        return carry

    bodies = max(n - 2, 0) // PIPELINE_UNROLL
    lax.fori_loop(0, bodies, body, 0)
    for i in range(2 + bodies * PIPELINE_UNROLL, n):
        step(i, i % 2)
    if n >= 2:
        stage_c(n - 2, n % 2)
        stage_b(n - 1, (n - 1) % 2)
    stage_c(n - 1, (n - 1) % 2)


def _suffix_sum_matrix():
    j = jnp.arange(2 * LANES)[:, None] % LANES
    n = jnp.arange(2 * LANES)[None, :]
    return jnp.where(n < LANES, (j >= n), True).astype(F32)


def _sb_kernel(item_q_ref, item_t_ref, q_ref, k_ref, v_ref, tri_ref, bias_ref, o_ref,
               qh_ref, c_ref, acc_ref, z_ref, w_ref, *, tq, tk, n_pairs, n_items):
    lane = lax.broadcasted_iota(jnp.int32, (tq, LANES), 1)
    n_sub = tk // LANES
    n_chains = 2 * n_pairs

    for p in range(n_pairs):
        qp = q_ref[:, p * LANES:(p + 1) * LANES]
        q_lane = lax.broadcasted_iota(jnp.int32, qp.shape, 1)
        for h in range(2):
            head_lanes = (q_lane < SB_HEAD_DIM) if h == 0 else (q_lane >= SB_HEAD_DIM)
            qh_ref[2 * p + h] = jnp.where(head_lanes, qp, jnp.zeros_like(qp))
    c_ref[...] = jnp.zeros_like(c_ref)
    acc_ref[...] = jnp.zeros_like(acc_ref)

    def item(f):
        qi = item_q_ref[f]
        t = item_t_ref[f]
        q_rows = pl.ds(pl.multiple_of(qi * tq, tq), tq)
        k_rows = pl.ds(pl.multiple_of((qi - t) * tk, tk), tk)
        return q_rows, k_rows, t

    def pair_lanes(ci):
        return slice((ci // 2) * LANES, (ci // 2 + 1) * LANES)

    def scores(f, slot):
        q_rows, k_rows, t = item(f)
        bias = bias_ref[(t == 0).astype(jnp.int32)]
        for ci in range(n_chains):
            z_ref[slot, ci] = _dot_t(qh_ref[ci, q_rows, :], k_ref[k_rows, pair_lanes(ci)]) + bias

    def weights(f, slot):
        _, _, t = item(f)
        keep = jnp.where(t == 0, 0.0, 1.0)
        tri = tri_ref[...]
        sign = jnp.uint32(0x80000000)
        bf16_bits = jnp.uint32(0xFFFF0000)
        for ci in range(n_chains):
            c = c_ref[ci] * keep
            for s in reversed(range(n_sub)):
                z = z_ref[slot, ci, :, s * LANES:(s + 1) * LANES]
                neg_abs = pltpu.bitcast(pltpu.bitcast(z, jnp.uint32) | sign, F32)
                sp = jnp.maximum(z, 0.0) + jnp.log2(1.0 + jnp.exp2(neg_abs))
                hi = pltpu.bitcast(pltpu.bitcast(sp, jnp.uint32) & bf16_bits, F32)
                r = _dot(jnp.concatenate([hi, sp - hi], axis=1), tri)
                w_ref[slot, ci, :, s * LANES:(s + 1) * LANES] = jnp.exp2(z - (c + r[:, :LANES])).astype(BF16)
                c = c + r[:, LANES:]
            c_ref[ci] = c

    def values(f, slot):
        q_rows, k_rows, t = item(f)
        keep = jnp.where(t == 0, 0.0, 1.0)
        for p in range(n_pairs):
            accs = []
            for ci in (2 * p, 2 * p + 1):
                acc = acc_ref[ci] * keep + _dot(w_ref[slot, ci], v_ref[k_rows, pair_lanes(ci)])
                acc_ref[ci] = acc
                accs.append(acc)
            o_ref[q_rows, p * LANES:(p + 1) * LANES] = jnp.where(lane < SB_HEAD_DIM, accs[0], accs[1]).astype(o_ref.dtype)

    _pipeline3(n_items, scores, weights, values, newest_first=True)


def _sb_attention(sb3, tri):
    bsz, seq, _ = sb3.shape
    tq = min(ATTN_Q, seq)
    tk = tq
    n_pairs = SB_PAIRS_PER_STEP
    width = n_pairs * LANES
    groups = SB_WIDTH // width
    item_q, item_t = _item_schedule(seq // tq, 1)
    bias = _causal_bias(tq, tk, strict=True)

    def seq_block(first_block):
        return pl.BlockSpec((None, seq, width), lambda b, g, iq, it: (b, 0, first_block + g))

    grid_spec = pltpu.PrefetchScalarGridSpec(
        num_scalar_prefetch=2,
        grid=(bsz, groups),
        in_specs=[seq_block(0), seq_block(groups), seq_block(2 * groups),
                  pl.BlockSpec(tri.shape, lambda b, g, iq, it: (0, 0)),
                  pl.BlockSpec(bias.shape, lambda b, g, iq, it: (0, 0, 0))],
        out_specs=seq_block(0),
        scratch_shapes=[pltpu.VMEM((2 * n_pairs, seq, LANES), BF16),
                        pltpu.VMEM((2 * n_pairs, tq, LANES), F32),
                        pltpu.VMEM((2 * n_pairs, tq, LANES), F32),
                        pltpu.VMEM((2, 2 * n_pairs, tq, tk), F32),
                        pltpu.VMEM((2, 2 * n_pairs, tq, tk), BF16)],
    )
    return pl.pallas_call(
        functools.partial(_sb_kernel, tq=tq, tk=tk, n_pairs=n_pairs, n_items=len(item_q)),
        grid_spec=grid_spec,
        out_shape=jax.ShapeDtypeStruct((bsz, seq, SB_WIDTH), BF16),
        compiler_params=_cparams(("arbitrary", "arbitrary")),
        name="sb_attn",
    )(item_q, item_t, sb3, sb3, sb3, tri, bias)


def _mla_kernel(item_q_ref, item_t_ref, q_ref, k_ref, v_ref, bias_ref, o_ref,
                m_ref, s_ref, p_ref, alpha_ref, *, tq, tk, n_heads, n_items):
    n_sub = tk // LANES
    m_ref[...] = jnp.zeros_like(m_ref)
    o_ref[...] = jnp.zeros_like(o_ref)

    q_per_k = tk // tq

    def item(f):
        qi = item_q_ref[f]
        t = item_t_ref[f]
        q_rows = pl.ds(pl.multiple_of(qi * tq, tq), tq)
        k_rows = pl.ds(pl.multiple_of((qi // q_per_k - t) * tk, tk), tk)
        return q_rows, k_rows, t, qi

    def head_lanes(ci):
        return slice(ci * MLA_HEAD_LANES, (ci + 1) * MLA_HEAD_LANES)

    def scores(f, slot):
        q_rows, k_rows, t, qi = item(f)
        bias = bias_ref[jnp.where(t == 0, 1 + qi % q_per_k, 0)]
        for ci in range(n_heads):
            s_ref[slot, ci] = _dot_t(q_ref[q_rows, head_lanes(ci)], k_ref[k_rows, head_lanes(ci)]) + bias

    def numerators(f, slot):
        _, _, t, _ = item(f)
        for ci in range(n_heads):
            row_max = jnp.max(s_ref[slot, ci], axis=-1, keepdims=True)
            m = jnp.where(t == 0, -jnp.inf, m_ref[ci])
            m_new = jnp.maximum(m, jnp.broadcast_to(row_max, (tq, LANES)))
            alpha_ref[slot, ci] = jnp.exp2(m - m_new)
            for j in range(n_sub):
                sl = slice(j * LANES, (j + 1) * LANES)
                p_ref[slot, ci, :, sl] = jnp.exp2(s_ref[slot, ci, :, sl] - m_new).astype(BF16)
            m_ref[ci] = m_new

    def values(f, slot):
        q_rows, k_rows, _, _ = item(f)
        for ci in range(n_heads):
            acc = o_ref[q_rows, head_lanes(ci)]
            o_ref[q_rows, head_lanes(ci)] = (alpha_ref[slot, ci] * acc
                                             + _dot(p_ref[slot, ci], v_ref[k_rows, head_lanes(ci)]))

    _pipeline3(n_items, scores, numerators, values, newest_first=False)


def _mla_attention(qm3, km3, vm3):
    bsz, seq, _ = qm3.shape
    tq = min(ATTN_Q, seq)
    tk = min(MLA_ATTN_K, seq)
    n_heads = MLA_HEADS_PER_STEP
    groups = MLA_HEADS // n_heads
    width = n_heads * MLA_HEAD_LANES
    item_q, item_t = _item_schedule(seq // tq, tk // tq)
    bias = _causal_bias(tq, tk, strict=False)
    seq_block = pl.BlockSpec((None, seq, width), lambda b, g, iq, it: (b, 0, g))
    grid_spec = pltpu.PrefetchScalarGridSpec(
        num_scalar_prefetch=2,
        grid=(bsz, groups),
        in_specs=[seq_block, seq_block, seq_block, pl.BlockSpec(bias.shape, lambda b, g, iq, it: (0, 0, 0))],
        out_specs=seq_block,
        scratch_shapes=[pltpu.VMEM((n_heads, tq, LANES), F32),
                        pltpu.VMEM((2, n_heads, tq, tk), F32),
                        pltpu.VMEM((2, n_heads, tq, tk), BF16),
                        pltpu.VMEM((2, n_heads, tq, LANES), F32)],
    )
    return pl.pallas_call(
        functools.partial(_mla_kernel, tq=tq, tk=tk, n_heads=n_heads, n_items=len(item_q)),
        grid_spec=grid_spec,
        out_shape=jax.ShapeDtypeStruct((bsz, seq, MLA_HEADS * MLA_HEAD_LANES), F32),
        compiler_params=_cparams(("arbitrary", "arbitrary")),
        name="mla_attn",
    )(item_q, item_t, qm3, km3, vm3, bias)


def _layer_norm(v, g, b):
    mu = jnp.mean(v, axis=-1, keepdims=True)
    dv = v - mu
    var = jnp.mean(dv * dv, axis=-1, keepdims=True)
    return dv * lax.rsqrt(var + LN_EPS) * g + b


ROUTER_NO_LANE = 1 << 20


def _router_logits(u2, wr_ref, br_ref):
    return _dot(u2.astype(BF16), wr_ref[...]) + br_ref[...]


def _group_lanes(lane):
    return (lane >= N_EXPERTS) & (lane < N_EXPERTS + N_GROUPS)


def _merge_kernel(osb_ref, omla_ref, g_ref, x_ref, gate1_ref, scale2_ref, shift2_ref,
                  wsb_ref, wmla_ref, wout_ref, ln_g_ref, ln_b_ref, wr_ref, br_ref,
                  x1_ref, u2_ref, gidx_ref):
    d = x_ref.shape[-1]
    y_sb = _dot(osb_ref[...], wsb_ref[...])
    lane = lax.broadcasted_iota(jnp.int32, (osb_ref.shape[0], LANES), 1)
    o_mla = []
    for p in range(MLA_HEADS // 2):
        res = []
        for h in range(2):
            acc = omla_ref[:, (2 * p + h) * MLA_HEAD_LANES:(2 * p + h + 1) * MLA_HEAD_LANES]
            res.append(acc / pltpu.roll(acc, MLA_V_DIM, 1))
        o_mla.append(jnp.where(lane < MLA_V_DIM, res[0], pltpu.roll(res[1], MLA_V_DIM, 1)).astype(BF16))
    y_mla = _dot(jnp.concatenate(o_mla, axis=1), wmla_ref[...])
    gates = g_ref[...].astype(F32)
    mixed = jax.nn.sigmoid(gates[:, :d]) * y_sb + jax.nn.sigmoid(gates[:, d:]) * y_mla
    attn = _dot(mixed.astype(BF16), wout_ref[...])
    x1 = _layer_norm(DEEPNORM_ALPHA * x_ref[...] + gate1_ref[...] * attn, ln_g_ref[...], ln_b_ref[...])
    x1_ref[...] = x1
    u2 = x1 * (1.0 + scale2_ref[...]) + shift2_ref[...]
    u2_ref[...] = u2
    logits = _router_logits(u2, wr_ref, br_ref)
    lane = lax.broadcasted_iota(jnp.int32, logits.shape, 1)
    g_log = jnp.where(_group_lanes(lane), logits, -jnp.inf)
    g_max = jnp.max(g_log, axis=-1, keepdims=True)
    gidx_ref[...] = jnp.min(jnp.where(g_log == g_max, lane, ROUTER_NO_LANE), axis=-1, keepdims=True) - N_EXPERTS


def _merge(o_sb, o_mla, gates, x2, mod4, w_bsb, w_bmla, w_out, ln_g, ln_b, w_r, b_r, seq):
    n, d = x2.shape
    tm = min(MERGE_ROWS, seq)
    tiles_per_seq = seq // tm

    def whole(a):
        return pl.BlockSpec(a.shape, lambda i: (0,) * a.ndim)

    def mod_spec(k):
        return pl.BlockSpec((None, None, 1, d), lambda i: (i // tiles_per_seq, k, 0, 0))

    def rows(width):
        return pl.BlockSpec((tm, width), lambda i: (i, 0))

    return pl.pallas_call(
        _merge_kernel,
        grid=(n // tm,),
        in_specs=[rows(SB_WIDTH), rows(MLA_HEADS * MLA_HEAD_LANES), rows(2 * d), rows(d),
                  mod_spec(2), mod_spec(4), mod_spec(3),
                  whole(w_bsb), whole(w_bmla), whole(w_out), whole(ln_g), whole(ln_b), whole(w_r), whole(b_r)],
        out_specs=[rows(d), rows(d), rows(1)],
        out_shape=[jax.ShapeDtypeStruct((n, d), F32),
                   jax.ShapeDtypeStruct((n, d), F32),
                   jax.ShapeDtypeStruct((n, 1), jnp.int32)],
        compiler_params=_cparams(("arbitrary",)),
        name="merge",
    )(o_sb, o_mla, gates, x2, mod4, mod4, mod4, w_bsb, w_bmla, w_out, ln_g, ln_b, w_r, b_r)


def _route_plan(gidx, tm):
    n = gidx.shape[0]
    max_tiles = n // tm + N_GROUPS
    order = jnp.argsort(gidx, stable=True).astype(jnp.int32)
    counts = jnp.sum(gidx[:, None] == jnp.arange(N_GROUPS, dtype=jnp.int32)[None, :], axis=0).astype(jnp.int32)
    tiles = (counts + tm - 1) // tm
    tile_end = jnp.cumsum(tiles)
    tile_start = tile_end - tiles
    run_start = jnp.cumsum(counts) - counts
    tile_ids = jnp.arange(max_tiles, dtype=jnp.int32)
    n_tiles = tile_end[-1]
    tile_group = jnp.minimum(jnp.sum(tile_ids[:, None] >= tile_end[None, :], axis=1), N_GROUPS - 1).astype(jnp.int32)
    rows_before = (tile_ids - tile_start[tile_group]) * tm
    tile_valid = jnp.clip(counts[tile_group] - rows_before, 0, tm)
    tile_valid = jnp.where(tile_ids < n_tiles, tile_valid, 0).astype(jnp.int32)
    row = jnp.arange(tm, dtype=jnp.int32)[None, :]
    sorted_pos = (run_start[tile_group] + rows_before)[:, None] + row
    slot_token = jnp.where(row < tile_valid[:, None], order[jnp.clip(sorted_pos, 0, n - 1)], 0)
    return tile_group, tile_valid, n_tiles.reshape(1).astype(jnp.int32), slot_token.reshape(max_tiles, 1, tm)


def _experts_kernel(tile_group_ref, tile_valid_ref, n_tiles_ref, tok_ref, tok_next_ref,
                    u2_hbm, wr_ref, br_ref, wg_ref, wu_ref, wd_ref, y_hbm,
                    xbuf, ybuf, gather_sem, scatter_sem, *, tm):
    i = pl.program_id(0)
    n_tiles = n_tiles_ref[0]
    buf = i % 2

    def gather_row(tok, r, b):
        return pltpu.make_async_copy(u2_hbm.at[pl.ds(tok, 1)], xbuf.at[b, pl.ds(r, 1)], gather_sem.at[b])

    def scatter_row(tok, r):
        return pltpu.make_async_copy(ybuf.at[pl.ds(r, 1)], y_hbm.at[pl.ds(tok, 1)], scatter_sem)

    def start_gather(idx_ref, b):
        def body(r, carry):
            gather_row(idx_ref[0, r], r, b).start()
            return carry
        lax.fori_loop(0, tm, body, 0, unroll=DMA_ISSUE_UNROLL)

    def start_scatter(n_rows):
        def body(r, carry):
            scatter_row(tok_ref[0, r], r).start()
            return carry

        @pl.when(n_rows == tm)
        def _():
            lax.fori_loop(0, tm, body, 0, unroll=DMA_ISSUE_UNROLL)

        @pl.when(n_rows != tm)
        def _():
            lax.fori_loop(0, n_rows, body, 0)

    def wait_scatter(n_rows):
        def body(r, carry):
            scatter_row(0, r).wait()
            return carry

        @pl.when(n_rows == tm)
        def _():
            pltpu.make_async_copy(ybuf, ybuf, scatter_sem).wait()

        @pl.when(n_rows != tm)
        def _():
            lax.fori_loop(0, n_rows, body, 0)

    @pl.when(i == 0)
    def _():
        start_gather(tok_ref, 0)

    @pl.when(i < n_tiles)
    def _():
        pltpu.make_async_copy(xbuf.at[buf], xbuf.at[buf], gather_sem.at[buf]).wait()

        @pl.when(i + 1 < n_tiles)
        def _():
            start_gather(tok_next_ref, 1 - buf)

        group = tile_group_ref[i]
        x = xbuf[buf]
        logits = _router_logits(x, wr_ref, br_ref)
        lane = lax.broadcasted_iota(jnp.int32, logits.shape, 1)
        neg = -jnp.inf
        g_log = jnp.where(_group_lanes(lane), logits, neg)
        g_max = jnp.max(g_log, axis=-1, keepdims=True)
        p_group = 1.0 / jnp.sum(jnp.exp(g_log - g_max), axis=-1, keepdims=True)
        first = group * EXPERTS_PER_GROUP
        e_log = jnp.where((lane >= first) & (lane < first + EXPERTS_PER_GROUP), logits, neg)
        v1 = jnp.max(e_log, axis=-1, keepdims=True)
        i1 = jnp.min(jnp.where(e_log == v1, lane, ROUTER_NO_LANE), axis=-1, keepdims=True)
        e_log2 = jnp.where(lane == i1, neg, e_log)
        v2 = jnp.max(e_log2, axis=-1, keepdims=True)
        i2 = jnp.min(jnp.where(e_log2 == v2, lane, ROUTER_NO_LANE), axis=-1, keepdims=True)
        e2 = jnp.exp(v2 - v1)
        w1 = p_group / (1.0 + e2)
        w2 = p_group * e2 / (1.0 + e2)
        combine = jnp.where(lane == i1, w1, 0.0) + jnp.where(lane == i2, w2, 0.0)

        xb = x.astype(BF16)
        hidden = []
        for e in range(EXPERTS_PER_GROUP):
            w_tok = jnp.sum(jnp.where(lane == first + e, combine, 0.0), axis=-1, keepdims=True)
            hg = _dot(xb, wg_ref[e])
            hu = _dot(xb, wu_ref[e])
            hidden.append((hg * jax.nn.sigmoid(hg) * hu * w_tok).astype(BF16))
        y = _dot(jnp.concatenate(hidden, axis=1), wd_ref[...])

        @pl.when(i >= 1)
        def _():
            wait_scatter(tile_valid_ref[jnp.maximum(i - 1, 0)])

        ybuf[...] = y
        n_valid = tile_valid_ref[i]
        start_scatter(n_valid)

        @pl.when(i == n_tiles - 1)
        def _():
            wait_scatter(n_valid)


def _experts(u2, gidx, w_r, b_r, w_g, w_u, w_d):
    n, d = u2.shape
    tm = MOE_ROWS
    d_e = w_g.shape[-1]
    tile_group, tile_valid, n_tiles, slot_token = _route_plan(gidx, tm)
    max_tiles = slot_token.shape[0]
    w_g = w_g.reshape(N_GROUPS, EXPERTS_PER_GROUP, d, d_e)
    w_u = w_u.reshape(N_GROUPS, EXPERTS_PER_GROUP, d, d_e)
    w_d = w_d.reshape(N_GROUPS, EXPERTS_PER_GROUP * d_e, d)
    grid_spec = pltpu.PrefetchScalarGridSpec(
        num_scalar_prefetch=3,
        grid=(max_tiles,),
        in_specs=[pl.BlockSpec((None, 1, tm), lambda i, tg, tv, nt: (i, 0, 0), memory_space=pltpu.SMEM),
                  pl.BlockSpec((None, 1, tm), lambda i, tg, tv, nt: (jnp.minimum(i + 1, max_tiles - 1), 0, 0),
                               memory_space=pltpu.SMEM),
                  pl.BlockSpec(memory_space=pl.ANY),
                  pl.BlockSpec(w_r.shape, lambda i, tg, tv, nt: (0, 0)),
                  pl.BlockSpec(b_r.shape, lambda i, tg, tv, nt: (0, 0)),
                  pl.BlockSpec((None, EXPERTS_PER_GROUP, d, d_e), lambda i, tg, tv, nt: (tg[i], 0, 0, 0)),
                  pl.BlockSpec((None, EXPERTS_PER_GROUP, d, d_e), lambda i, tg, tv, nt: (tg[i], 0, 0, 0)),
                  pl.BlockSpec((None, EXPERTS_PER_GROUP * d_e, d), lambda i, tg, tv, nt: (tg[i], 0, 0))],
        out_specs=pl.BlockSpec(memory_space=pl.ANY),
        scratch_shapes=[pltpu.VMEM((2, tm, d), F32),
                        pltpu.VMEM((tm, d), F32),
                        pltpu.SemaphoreType.DMA((2,)),
                        pltpu.SemaphoreType.DMA(())],
    )
    return pl.pallas_call(
        functools.partial(_experts_kernel, tm=tm),
        grid_spec=grid_spec,
        out_shape=jax.ShapeDtypeStruct((n, d), F32),
        compiler_params=_cparams(("arbitrary",)),
        name="experts",
    )(tile_group, tile_valid, n_tiles, slot_token, slot_token, u2, w_r, b_r, w_g, w_u, w_d)


def _ln2_kernel(x1_ref, y_ref, gate2_ref, ln_g_ref, ln_b_ref, o_ref):
    v = DEEPNORM_ALPHA * x1_ref[...] + gate2_ref[...] * y_ref[...]
    o_ref[...] = _layer_norm(v, ln_g_ref[...], ln_b_ref[...])


def _ln2(x1, y, mod4, ln_g, ln_b, seq):
    n, d = x1.shape
    tm = min(LN2_ROWS, seq)
    tiles_per_seq = seq // tm
    rows = pl.BlockSpec((tm, d), lambda i: (i, 0))
    vec = pl.BlockSpec((1, d), lambda i: (0, 0))
    return pl.pallas_call(
        _ln2_kernel,
        grid=(n // tm,),
        in_specs=[rows, rows, pl.BlockSpec((None, None, 1, d), lambda i: (i // tiles_per_seq, 5, 0, 0)), vec, vec],
        out_specs=rows,
        out_shape=jax.ShapeDtypeStruct((n, d), F32),
        compiler_params=_cparams(("arbitrary",)),
        name="ln2",
    )(x1, y, mod4, ln_g, ln_b)


def kernel(x, c, positions, w_ada, b_ada, w_in, mla_q_norm_g, w_q_up, mla_kv_norm_g, w_kv_up, w_branch_sb, w_branch_mla, w_out, ln1_g, ln1_b, w_router_group, b_router_group, w_router_expert, b_router_expert, w_exp_gate, w_exp_up, w_exp_down, ln2_g, ln2_b):
    bsz, seq, d = x.shape
    n = bsz * seq
    invf = _rope_inv_freq_tile()
    posf = positions.astype(F32).reshape(n, 1)
    tri = _suffix_sum_matrix()
    for l in range(w_ada.shape[0]):
        x2 = x.reshape(n, d)
        mod4 = _adaln_mod(c, w_ada[l], b_ada[l]).reshape(bsz, N_MOD, 1, d)
        weights = _prep_in_weights(w_in[l], w_q_up[l], w_kv_up[l])
        sb, qm, km, vm, gates = _in_proj(x2, mod4, posf, invf, weights,
                                         mla_q_norm_g[l].reshape(1, -1), mla_kv_norm_g[l].reshape(1, -1), seq)
        o_sb = _sb_attention(sb.reshape(bsz, seq, -1), tri).reshape(n, SB_WIDTH)
        o_mla = _mla_attention(qm.reshape(bsz, seq, -1), km.reshape(bsz, seq, -1),
                               vm.reshape(bsz, seq, -1)).reshape(n, -1)
        w_r = jnp.zeros((d, LANES), F32)
        w_r = w_r.at[:, :N_EXPERTS].set(w_router_expert[l]).at[:, N_EXPERTS:N_EXPERTS + N_GROUPS].set(w_router_group[l])
        b_r = jnp.zeros((1, LANES), F32)
        b_r = b_r.at[0, :N_EXPERTS].set(b_router_expert[l]).at[0, N_EXPERTS:N_EXPERTS + N_GROUPS].set(b_router_group[l])
        w_r = w_r.astype(BF16)
        x1, u2, gidx = _merge(o_sb, o_mla, gates, x2, mod4,
                              w_branch_sb[l].astype(BF16), w_branch_mla[l].astype(BF16), w_out[l].astype(BF16),
                              ln1_g[l].reshape(1, d), ln1_b[l].reshape(1, d), w_r, b_r, seq)
        y = _experts(u2, gidx.reshape(n), w_r, b_r, w_exp_gate[l].astype(BF16), w_exp_up[l].astype(BF16),
                     w_exp_down[l].astype(BF16))
        out = _ln2(x1, y, mod4, ln2_g[l].reshape(1, d), ln2_b[l].reshape(1, d), seq)
        x = out.reshape(bsz, seq, d)
    return x
```

```python
import functools
import math

import jax
import jax.numpy as jnp
import numpy as np
from jax import lax
from jax.experimental import pallas as pl
from jax.experimental.pallas import tpu as pltpu

F32 = jnp.float32
BF16 = jnp.bfloat16

D_MODEL = 1024
SB_HEADS = 8
SB_HEAD_DIM = 64
SB_WIDTH = SB_HEADS * SB_HEAD_DIM
MLA_HEADS = 8
MLA_NOPE_DIM = 64
MLA_ROPE_DIM = 32
MLA_V_DIM = 64
MLA_Q_RANK = 384
MLA_KV_RANK = 256
MLA_QK_DIM = MLA_NOPE_DIM + MLA_ROPE_DIM
MLA_WIDTH = MLA_HEADS * MLA_V_DIM
ROPE_THETA = 10000.0
N_GROUPS = 4
EXPERTS_PER_GROUP = 8
N_EXPERTS = N_GROUPS * EXPERTS_PER_GROUP
D_EXPERT = 256
DEPTH = 1
DEEPNORM_ALPHA = (2.0 * DEPTH) ** 0.25
LN_EPS = 1e-5
RMS_EPS = 1e-6
LOG2_E = math.log2(math.e)
MASK_BIAS = -1e30
N_MOD = 6

LANES = 128
MLA_HEAD_LANES = 128
VMEM_LIMIT_BYTES = 56 * 1024 * 1024

PROJ_ROWS = 512
ATTN_Q = 256
MLA_ATTN_K = 512
MERGE_ROWS = 512
MOE_ROWS = 512
LN2_ROWS = 512
DMA_ISSUE_UNROLL = 8
SB_PAIRS_PER_STEP = 2
MLA_HEADS_PER_STEP = 4
PIPELINE_UNROLL = 4


def _cparams(sem):
    return pltpu.CompilerParams(dimension_semantics=sem, vmem_limit_bytes=VMEM_LIMIT_BYTES)


def _dot(a, b):
    return jnp.dot(a, b, preferred_element_type=F32)


def _dot_t(a, b):
    return lax.dot_general(a, b, (((1,), (1,)), ((), ())), preferred_element_type=F32)


def _adaln_kernel(c_ref, w_ref, b_ref, o_ref):
    c = c_ref[...]
    ca = c * jax.nn.sigmoid(c)
    o_ref[...] = _dot(ca.astype(BF16), w_ref[...].astype(BF16)) + b_ref[...]


def _adaln_mod(c, w_ada, b_ada):
    bsz, d = c.shape
    return pl.pallas_call(
        _adaln_kernel,
        grid=(N_MOD,),
        in_specs=[pl.BlockSpec((bsz, d), lambda j: (0, 0)),
                  pl.BlockSpec((d, d), lambda j: (0, j)),
                  pl.BlockSpec((1, d), lambda j: (0, j))],
        out_specs=pl.BlockSpec((bsz, d), lambda j: (0, j)),
        out_shape=jax.ShapeDtypeStruct((bsz, N_MOD * d), F32),
        compiler_params=_cparams(("arbitrary",)),
        name="adaln_mod",
    )(c, w_ada, b_ada.reshape(1, N_MOD * d))


def _in_proj_kernel(x_ref, scale_ref, shift_ref, pos_ref, invf_ref,
                    w_sb_ref, w_dn_ref, w_kr_ref, w_g_ref, qg_ref, kvg_ref,
                    w_qu_ref, w_kvu_ref,
                    sb_ref, qm_ref, km_ref, vm_ref, g_ref):
    u = (x_ref[...] * (1.0 + scale_ref[...]) + shift_ref[...]).astype(BF16)

    sb = _dot(u, w_sb_ref[...])
    sb_scale = LOG2_E / math.sqrt(SB_HEAD_DIM)
    sb_ref[:, :SB_WIDTH] = (sb[:, :SB_WIDTH] * sb_scale).astype(BF16)
    sb_ref[:, SB_WIDTH:] = sb[:, SB_WIDTH:].astype(BF16)

    g_ref[...] = _dot(u, w_g_ref[...]).astype(BF16)

    ang = pos_ref[...] * invf_ref[...]
    cos = jnp.cos(ang)
    sin = jnp.sin(ang)

    dn = _dot(u, w_dn_ref[...])
    q_dn = dn[:, :MLA_Q_RANK]
    kv_dn = dn[:, MLA_Q_RANK:]
    qn = q_dn * lax.rsqrt(jnp.mean(q_dn * q_dn, axis=-1, keepdims=True) + RMS_EPS) * qg_ref[...]
    kvn = kv_dn * lax.rsqrt(jnp.mean(kv_dn * kv_dn, axis=-1, keepdims=True) + RMS_EPS) * kvg_ref[...]
    qn = qn.astype(BF16)
    kvn = kvn.astype(BF16)

    width = MLA_HEADS * MLA_HEAD_LANES
    qu = _dot(qn, w_qu_ref[...])
    kvu = _dot(kvn, w_kvu_ref[...])
    kr = _dot(u, w_kr_ref[...])
    kr = kr[:, :LANES] * cos + kr[:, LANES:] * sin
    lane = lax.broadcasted_iota(jnp.int32, (u.shape[0], LANES), 1)
    mla_scale = LOG2_E / math.sqrt(MLA_QK_DIM)
    heads_per_rot_tile = LANES // MLA_ROPE_DIM
    for h in range(MLA_HEADS):
        sl = slice(h * MLA_HEAD_LANES, (h + 1) * MLA_HEAD_LANES)
        rot_tile = qu[:, width + (h // heads_per_rot_tile) * LANES:width + (h // heads_per_rot_tile + 1) * LANES]
        shift = (MLA_NOPE_DIM - (h % heads_per_rot_tile) * MLA_ROPE_DIM) % LANES
        rot = pltpu.roll(rot_tile, shift, 1) if shift else rot_tile
        qm_ref[:, sl] = ((qu[:, sl] * cos + rot * sin) * mla_scale).astype(BF16)
        kv_h = kvu[:, sl]
        km_ref[:, sl] = (jnp.where(lane < MLA_NOPE_DIM, kv_h, 0.0) + kr).astype(BF16)
        vm_ref[:, sl] = jnp.where(lane < MLA_V_DIM, pltpu.roll(kv_h, MLA_V_DIM, 1), 1.0).astype(BF16)


def _rope_inv_freq_tile():
    inv_freq = 1.0 / (ROPE_THETA ** (jnp.arange(0, MLA_ROPE_DIM, 2, dtype=F32) / MLA_ROPE_DIM))
    tile = jnp.zeros((LANES,), F32)
    half = MLA_ROPE_DIM // 2
    tile = tile.at[MLA_NOPE_DIM:MLA_NOPE_DIM + half].set(inv_freq)
    tile = tile.at[MLA_NOPE_DIM + half:MLA_NOPE_DIM + 2 * half].set(inv_freq)
    return tile.reshape(1, LANES)


def _rotate_half_cols(w):
    half = MLA_ROPE_DIM // 2
    return jnp.concatenate([-w[..., half:], w[..., :half]], axis=-1)


def _prep_in_weights(w_in, w_q_up, w_kv_up):
    o = 0
    w_sb = w_in[:, o:o + 3 * SB_WIDTH]; o += 3 * SB_WIDTH
    w_dn = w_in[:, o:o + MLA_Q_RANK + MLA_KV_RANK]; o += MLA_Q_RANK + MLA_KV_RANK
    w_kr = w_in[:, o:o + MLA_ROPE_DIM]; o += MLA_ROPE_DIM
    w_g = w_in[:, o:]

    def rope_tile(w):
        z = jnp.zeros((w.shape[0], LANES), w.dtype)
        return z.at[:, MLA_NOPE_DIM:MLA_NOPE_DIM + MLA_ROPE_DIM].set(w)

    w_kr2 = jnp.concatenate([rope_tile(w_kr), rope_tile(_rotate_half_cols(w_kr))], axis=1)

    wq = w_q_up.reshape(MLA_Q_RANK, MLA_HEADS, MLA_QK_DIM)
    q_base = jnp.zeros((MLA_Q_RANK, MLA_HEADS, MLA_HEAD_LANES), w_q_up.dtype).at[:, :, :MLA_QK_DIM].set(wq)
    q_rot = _rotate_half_cols(wq[:, :, MLA_NOPE_DIM:])
    w_qu = jnp.concatenate([q_base.reshape(MLA_Q_RANK, -1), q_rot.reshape(MLA_Q_RANK, -1)], axis=1)
    return tuple(w.astype(BF16) for w in (w_sb, w_dn, w_kr2, w_g, w_qu, w_kv_up))


def _in_proj(x2, mod4, posf, invf, weights, qg, kvg, seq):
    n, d = x2.shape
    tm = min(PROJ_ROWS, seq)
    tiles_per_seq = seq // tm
    w_sb, w_dn, w_kr2, w_g, w_qu, w_kvu = weights

    def whole(a):
        return pl.BlockSpec(a.shape, lambda i: (0,) * a.ndim)

    def mod_spec(k):
        return pl.BlockSpec((None, None, 1, d), lambda i: (i // tiles_per_seq, k, 0, 0))

    def rows(width):
        return pl.BlockSpec((tm, width), lambda i: (i, 0))

    width = MLA_HEADS * MLA_HEAD_LANES
    return pl.pallas_call(
        _in_proj_kernel,
        grid=(n // tm,),
        in_specs=[rows(d), mod_spec(1), mod_spec(0), rows(1), whole(invf),
                  whole(w_sb), whole(w_dn), whole(w_kr2), whole(w_g), whole(qg), whole(kvg),
                  whole(w_qu), whole(w_kvu)],
        out_specs=[rows(3 * SB_WIDTH), rows(width), rows(width), rows(width), rows(2 * d)],
        out_shape=[jax.ShapeDtypeStruct((n, 3 * SB_WIDTH), BF16),
                   jax.ShapeDtypeStruct((n, width), BF16),
                   jax.ShapeDtypeStruct((n, width), BF16),
                   jax.ShapeDtypeStruct((n, width), BF16),
                   jax.ShapeDtypeStruct((n, 2 * d), BF16)],
        compiler_params=_cparams(("arbitrary",)),
        name="in_proj",
    )(x2, mod4, mod4, posf, invf, w_sb, w_dn, w_kr2, w_g, qg, kvg, w_qu, w_kvu)


def _item_schedule(n_q, q_per_k):
    q_of, t_of = [], []
    for qi in range(n_q):
        for t in range(qi // q_per_k + 1):
            q_of.append(qi)
            t_of.append(t)
    return np.asarray(q_of, np.int32), np.asarray(t_of, np.int32)


def _causal_bias(tq, tk, strict):
    c = jnp.arange(tk)[None, :]
    tiles = [jnp.zeros((tq, tk), F32)]
    for r in range(tk // tq):
        q_pos = jnp.arange(tq)[:, None] + r * tq
        visible = (c < q_pos) if strict else (c <= q_pos)
        tiles.append(jnp.where(visible, 0.0, MASK_BIAS).astype(F32))
    return jnp.stack(tiles)


def _pipeline3(n, stage_a, stage_b, stage_c, newest_first):
    stage_a(0, 0)
    if n >= 2:
        stage_a(1, 1)
    stage_b(0, 0)

    def step(i, slot):
        stages = [lambda: stage_a(i, slot), lambda: stage_b(i - 1, 1 - slot), lambda: stage_c(i - 2, slot)]
        for run in (stages if newest_first else reversed(stages)):
            run()

    def body(k, carry):
        for j in range(PIPELINE_UNROLL):
            step(2 + PIPELINE_UNROLL * k + j, j % 2)
        return carry

    bodies = max(n - 2, 0) // PIPELINE_UNROLL
    lax.fori_loop(0, bodies, body, 0)
    for i in range(2 + bodies * PIPELINE_UNROLL, n):
        step(i, i % 2)
    if n >= 2:
        stage_c(n - 2, n % 2)
        stage_b(n - 1, (n - 1) % 2)
    stage_c(n - 1, (n - 1) % 2)


def _suffix_sum_matrix():
    j = jnp.arange(2 * LANES)[:, None] % LANES
    n = jnp.arange(2 * LANES)[None, :]
    return jnp.where(n < LANES, (j >= n), True).astype(F32)


def _sb_kernel(item_q_ref, item_t_ref, q_ref, k_ref, v_ref, tri_ref, bias_ref, o_ref,
               qh_ref, c_ref, acc_ref, z_ref, w_ref, *, tq, tk, n_pairs, n_items):
    lane = lax.broadcasted_iota(jnp.int32, (tq, LANES), 1)
    n_sub = tk // LANES
    n_chains = 2 * n_pairs

    for p in range(n_pairs):
        qp = q_ref[:, p * LANES:(p + 1) * LANES]
        q_lane = lax.broadcasted_iota(jnp.int32, qp.shape, 1)
        for h in range(2):
            head_lanes = (q_lane < SB_HEAD_DIM) if h == 0 else (q_lane >= SB_HEAD_DIM)
            qh_ref[2 * p + h] = jnp.where(head_lanes, qp, jnp.zeros_like(qp))
    c_ref[...] = jnp.zeros_like(c_ref)
    acc_ref[...] = jnp.zeros_like(acc_ref)

    def item(f):
        qi = item_q_ref[f]
        t = item_t_ref[f]
        q_rows = pl.ds(pl.multiple_of(qi * tq, tq), tq)
        k_rows = pl.ds(pl.multiple_of((qi - t) * tk, tk), tk)
        return q_rows, k_rows, t

    def pair_lanes(ci):
        return slice((ci // 2) * LANES, (ci // 2 + 1) * LANES)

    def scores(f, slot):
        q_rows, k_rows, t = item(f)
        bias = bias_ref[(t == 0).astype(jnp.int32)]
        for ci in range(n_chains):
            z_ref[slot, ci] = _dot_t(qh_ref[ci, q_rows, :], k_ref[k_rows, pair_lanes(ci)]) + bias

    def weights(f, slot):
        _, _, t = item(f)
        keep = jnp.where(t == 0, 0.0, 1.0)
        tri = tri_ref[...]
        sign = jnp.uint32(0x80000000)
        bf16_bits = jnp.uint32(0xFFFF0000)
        for ci in range(n_chains):
            c = c_ref[ci] * keep
            for s in reversed(range(n_sub)):
                z = z_ref[slot, ci, :, s * LANES:(s + 1) * LANES]
                neg_abs = pltpu.bitcast(pltpu.bitcast(z, jnp.uint32) | sign, F32)
                sp = jnp.maximum(z, 0.0) + jnp.log2(1.0 + jnp.exp2(neg_abs))
                hi = pltpu.bitcast(pltpu.bitcast(sp, jnp.uint32) & bf16_bits, F32)
                r = _dot(jnp.concatenate([hi, sp - hi], axis=1), tri)
                w_ref[slot, ci, :, s * LANES:(s + 1) * LANES] = jnp.exp2(z - (c + r[:, :LANES])).astype(BF16)
                c = c + r[:, LANES:]
            c_ref[ci] = c

    def values(f, slot):
        q_rows, k_rows, t = item(f)
        keep = jnp.where(t == 0, 0.0, 1.0)
        for p in range(n_pairs):
            accs = []
            for ci in (2 * p, 2 * p + 1):
                acc = acc_ref[ci] * keep + _dot(w_ref[slot, ci], v_ref[k_rows, pair_lanes(ci)])
                acc_ref[ci] = acc
                accs.append(acc)
            o_ref[q_rows, p * LANES:(p + 1) * LANES] = jnp.where(lane < SB_HEAD_DIM, accs[0], accs[1]).astype(o_ref.dtype)

    _pipeline3(n_items, scores, weights, values, newest_first=True)


def _sb_attention(sb3, tri):
    bsz, seq, _ = sb3.shape
    tq = min(ATTN_Q, seq)
    tk = tq
    n_pairs = SB_PAIRS_PER_STEP
    width = n_pairs * LANES
    groups = SB_WIDTH // width
    item_q, item_t = _item_schedule(seq // tq, 1)
    bias = _causal_bias(tq, tk, strict=True)

    def seq_block(first_block):
        return pl.BlockSpec((None, seq, width), lambda b, g, iq, it: (b, 0, first_block + g))

    grid_spec = pltpu.PrefetchScalarGridSpec(
        num_scalar_prefetch=2,
        grid=(bsz, groups),
        in_specs=[seq_block(0), seq_block(groups), seq_block(2 * groups),
                  pl.BlockSpec(tri.shape, lambda b, g, iq, it: (0, 0)),
                  pl.BlockSpec(bias.shape, lambda b, g, iq, it: (0, 0, 0))],
        out_specs=seq_block(0),
        scratch_shapes=[pltpu.VMEM((2 * n_pairs, seq, LANES), BF16),
                        pltpu.VMEM((2 * n_pairs, tq, LANES), F32),
                        pltpu.VMEM((2 * n_pairs, tq, LANES), F32),
                        pltpu.VMEM((2, 2 * n_pairs, tq, tk), F32),
                        pltpu.VMEM((2, 2 * n_pairs, tq, tk), BF16)],
    )
    return pl.pallas_call(
        functools.partial(_sb_kernel, tq=tq, tk=tk, n_pairs=n_pairs, n_items=len(item_q)),
        grid_spec=grid_spec,
        out_shape=jax.ShapeDtypeStruct((bsz, seq, SB_WIDTH), BF16),
        compiler_params=_cparams(("arbitrary", "arbitrary")),
        name="sb_attn",
    )(item_q, item_t, sb3, sb3, sb3, tri, bias)


def _mla_kernel(item_q_ref, item_t_ref, q_ref, k_ref, v_ref, bias_ref, o_ref,
                m_ref, s_ref, p_ref, alpha_ref, *, tq, tk, n_heads, n_items):
    n_sub = tk // LANES
    m_ref[...] = jnp.zeros_like(m_ref)
    o_ref[...] = jnp.zeros_like(o_ref)

    q_per_k = tk // tq

    def item(f):
        qi = item_q_ref[f]
        t = item_t_ref[f]
        q_rows = pl.ds(pl.multiple_of(qi * tq, tq), tq)
        k_rows = pl.ds(pl.multiple_of((qi // q_per_k - t) * tk, tk), tk)
        return q_rows, k_rows, t, qi

    def head_lanes(ci):
        return slice(ci * MLA_HEAD_LANES, (ci + 1) * MLA_HEAD_LANES)

    def scores(f, slot):
        q_rows, k_rows, t, qi = item(f)
        bias = bias_ref[jnp.where(t == 0, 1 + qi % q_per_k, 0)]
        for ci in range(n_heads):
            s_ref[slot, ci] = _dot_t(q_ref[q_rows, head_lanes(ci)], k_ref[k_rows, head_lanes(ci)]) + bias

    def numerators(f, slot):
        _, _, t, _ = item(f)
        for ci in range(n_heads):
            row_max = jnp.max(s_ref[slot, ci], axis=-1, keepdims=True)
            m = jnp.where(t == 0, -jnp.inf, m_ref[ci])
            m_new = jnp.maximum(m, jnp.broadcast_to(row_max, (tq, LANES)))
            alpha_ref[slot, ci] = jnp.exp2(m - m_new)
            for j in range(n_sub):
                sl = slice(j * LANES, (j + 1) * LANES)
                p_ref[slot, ci, :, sl] = jnp.exp2(s_ref[slot, ci, :, sl] - m_new).astype(BF16)
            m_ref[ci] = m_new

    def values(f, slot):
        q_rows, k_rows, _, _ = item(f)
        for ci in range(n_heads):
            acc = o_ref[q_rows, head_lanes(ci)]
            o_ref[q_rows, head_lanes(ci)] = (alpha_ref[slot, ci] * acc
                                             + _dot(p_ref[slot, ci], v_ref[k_rows, head_lanes(ci)]))

    _pipeline3(n_items, scores, numerators, values, newest_first=False)


def _mla_attention(qm3, km3, vm3):
    bsz, seq, _ = qm3.shape
    tq = min(ATTN_Q, seq)
    tk = min(MLA_ATTN_K, seq)
    n_heads = MLA_HEADS_PER_STEP
    groups = MLA_HEADS // n_heads
    width = n_heads * MLA_HEAD_LANES
    item_q, item_t = _item_schedule(seq // tq, tk // tq)
    bias = _causal_bias(tq, tk, strict=False)
    seq_block = pl.BlockSpec((None, seq, width), lambda b, g, iq, it: (b, 0, g))
    grid_spec = pltpu.PrefetchScalarGridSpec(
        num_scalar_prefetch=2,
        grid=(bsz, groups),
        in_specs=[seq_block, seq_block, seq_block, pl.BlockSpec(bias.shape, lambda b, g, iq, it: (0, 0, 0))],
        out_specs=seq_block,
        scratch_shapes=[pltpu.VMEM((n_heads, tq, LANES), F32),
                        pltpu.VMEM((2, n_heads, tq, tk), F32),
                        pltpu.VMEM((2, n_heads, tq, tk), BF16),
                        pltpu.VMEM((2, n_heads, tq, LANES), F32)],
    )
    return pl.pallas_call(
        functools.partial(_mla_kernel, tq=tq, tk=tk, n_heads=n_heads, n_items=len(item_q)),
        grid_spec=grid_spec,
        out_shape=jax.ShapeDtypeStruct((bsz, seq, MLA_HEADS * MLA_HEAD_LANES), F32),
        compiler_params=_cparams(("arbitrary", "arbitrary")),
        name="mla_attn",
    )(item_q, item_t, qm3, km3, vm3, bias)


def _layer_norm(v, g, b):
    mu = jnp.mean(v, axis=-1, keepdims=True)
    dv = v - mu
    var = jnp.mean(dv * dv, axis=-1, keepdims=True)
    return dv * lax.rsqrt(var + LN_EPS) * g + b


ROUTER_NO_LANE = 1 << 20


def _router_logits(u2, wr_ref, br_ref):
    return _dot(u2.astype(BF16), wr_ref[...]) + br_ref[...]


def _group_lanes(lane):
    return (lane >= N_EXPERTS) & (lane < N_EXPERTS + N_GROUPS)


def _merge_kernel(osb_ref, omla_ref, g_ref, x_ref, gate1_ref, scale2_ref, shift2_ref,
                  wsb_ref, wmla_ref, wout_ref, ln_g_ref, ln_b_ref, wr_ref, br_ref,
                  x1_ref, u2_ref, gidx_ref):
    d = x_ref.shape[-1]
    y_sb = _dot(osb_ref[...], wsb_ref[...])
    lane = lax.broadcasted_iota(jnp.int32, (osb_ref.shape[0], LANES), 1)
    o_mla = []
    for p in range(MLA_HEADS // 2):
        res = []
        for h in range(2):
            acc = omla_ref[:, (2 * p + h) * MLA_HEAD_LANES:(2 * p + h + 1) * MLA_HEAD_LANES]
            res.append(acc / pltpu.roll(acc, MLA_V_DIM, 1))
        o_mla.append(jnp.where(lane < MLA_V_DIM, res[0], pltpu.roll(res[1], MLA_V_DIM, 1)).astype(BF16))
    y_mla = _dot(jnp.concatenate(o_mla, axis=1), wmla_ref[...])
    gates = g_ref[...].astype(F32)
    mixed = jax.nn.sigmoid(gates[:, :d]) * y_sb + jax.nn.sigmoid(gates[:, d:]) * y_mla
    attn = _dot(mixed.astype(BF16), wout_ref[...])
    x1 = _layer_norm(DEEPNORM_ALPHA * x_ref[...] + gate1_ref[...] * attn, ln_g_ref[...], ln_b_ref[...])
    x1_ref[...] = x1
    u2 = x1 * (1.0 + scale2_ref[...]) + shift2_ref[...]
    u2_ref[...] = u2
    logits = _router_logits(u2, wr_ref, br_ref)
    lane = lax.broadcasted_iota(jnp.int32, logits.shape, 1)
    g_log = jnp.where(_group_lanes(lane), logits, -jnp.inf)
    g_max = jnp.max(g_log, axis=-1, keepdims=True)
    gidx_ref[...] = jnp.min(jnp.where(g_log == g_max, lane, ROUTER_NO_LANE), axis=-1, keepdims=True) - N_EXPERTS


def _merge(o_sb, o_mla, gates, x2, mod4, w_bsb, w_bmla, w_out, ln_g, ln_b, w_r, b_r, seq):
    n, d = x2.shape
    tm = min(MERGE_ROWS, seq)
    tiles_per_seq = seq // tm

    def whole(a):
        return pl.BlockSpec(a.shape, lambda i: (0,) * a.ndim)

    def mod_spec(k):
        return pl.BlockSpec((None, None, 1, d), lambda i: (i // tiles_per_seq, k, 0, 0))

    def rows(width):
        return pl.BlockSpec((tm, width), lambda i: (i, 0))

    return pl.pallas_call(
        _merge_kernel,
        grid=(n // tm,),
        in_specs=[rows(SB_WIDTH), rows(MLA_HEADS * MLA_HEAD_LANES), rows(2 * d), rows(d),
                  mod_spec(2), mod_spec(4), mod_spec(3),
                  whole(w_bsb), whole(w_bmla), whole(w_out), whole(ln_g), whole(ln_b), whole(w_r), whole(b_r)],
        out_specs=[rows(d), rows(d), rows(1)],
        out_shape=[jax.ShapeDtypeStruct((n, d), F32),
                   jax.ShapeDtypeStruct((n, d), F32),
                   jax.ShapeDtypeStruct((n, 1), jnp.int32)],
        compiler_params=_cparams(("arbitrary",)),
        name="merge",
    )(o_sb, o_mla, gates, x2, mod4, mod4, mod4, w_bsb, w_bmla, w_out, ln_g, ln_b, w_r, b_r)


def _route_plan(gidx, tm):
    n = gidx.shape[0]
    max_tiles = n // tm + N_GROUPS
    order = jnp.argsort(gidx, stable=True).astype(jnp.int32)
    counts = jnp.sum(gidx[:, None] == jnp.arange(N_GROUPS, dtype=jnp.int32)[None, :], axis=0).astype(jnp.int32)
    tiles = (counts + tm - 1) // tm
    tile_end = jnp.cumsum(tiles)
    tile_start = tile_end - tiles
    run_start = jnp.cumsum(counts) - counts
    tile_ids = jnp.arange(max_tiles, dtype=jnp.int32)
    n_tiles = tile_end[-1]
    tile_group = jnp.minimum(jnp.sum(tile_ids[:, None] >= tile_end[None, :], axis=1), N_GROUPS - 1).astype(jnp.int32)
    rows_before = (tile_ids - tile_start[tile_group]) * tm
    tile_valid = jnp.clip(counts[tile_group] - rows_before, 0, tm)
    tile_valid = jnp.where(tile_ids < n_tiles, tile_valid, 0).astype(jnp.int32)
    row = jnp.arange(tm, dtype=jnp.int32)[None, :]
    sorted_pos = (run_start[tile_group] + rows_before)[:, None] + row
    slot_token = jnp.where(row < tile_valid[:, None], order[jnp.clip(sorted_pos, 0, n - 1)], 0)
    return tile_group, tile_valid, n_tiles.reshape(1).astype(jnp.int32), slot_token.reshape(max_tiles, 1, tm)


def _experts_kernel(tile_group_ref, tile_valid_ref, n_tiles_ref, tok_ref, tok_next_ref,
                    u2_hbm, wr_ref, br_ref, wg_ref, wu_ref, wd_ref, y_hbm,
                    xbuf, ybuf, gather_sem, scatter_sem, *, tm):
    i = pl.program_id(0)
    n_tiles = n_tiles_ref[0]
    buf = i % 2

    def gather_row(tok, r, b):
        return pltpu.make_async_copy(u2_hbm.at[pl.ds(tok, 1)], xbuf.at[b, pl.ds(r, 1)], gather_sem.at[b])

    def scatter_row(tok, r):
        return pltpu.make_async_copy(ybuf.at[pl.ds(r, 1)], y_hbm.at[pl.ds(tok, 1)], scatter_sem)

    def start_gather(idx_ref, b):
        def body(j, carry):
            for u in range(DMA_ISSUE_UNROLL):
                r = j * DMA_ISSUE_UNROLL + u
                gather_row(idx_ref[0, r], r, b).start(priority=u % 2)
            return carry
        lax.fori_loop(0, tm // DMA_ISSUE_UNROLL, body, 0)

    def start_scatter(n_rows):
        def body(r, carry):
            scatter_row(tok_ref[0, r], r).start()
            return carry

        def body_full(j, carry):
            for u in range(DMA_ISSUE_UNROLL):
                r = j * DMA_ISSUE_UNROLL + u
                scatter_row(tok_ref[0, r], r).start(priority=u % 2)
            return carry

        @pl.when(n_rows == tm)
        def _():
            lax.fori_loop(0, tm // DMA_ISSUE_UNROLL, body_full, 0)

        @pl.when(n_rows != tm)
        def _():
            lax.fori_loop(0, n_rows, body, 0)

    def wait_scatter(n_rows):
        def body(r, carry):
            scatter_row(0, r).wait()
            return carry

        @pl.when(n_rows == tm)
        def _():
            pltpu.make_async_copy(ybuf, ybuf, scatter_sem).wait()

        @pl.when(n_rows != tm)
        def _():
            lax.fori_loop(0, n_rows, body, 0)

    @pl.when(i == 0)
    def _():
        start_gather(tok_ref, 0)

    @pl.when(i < n_tiles)
    def _():
        pltpu.make_async_copy(xbuf.at[buf], xbuf.at[buf], gather_sem.at[buf]).wait()

        @pl.when(i + 1 < n_tiles)
        def _():
            start_gather(tok_next_ref, 1 - buf)

        group = tile_group_ref[i]
        x = xbuf[buf]
        logits = _router_logits(x, wr_ref, br_ref)
        lane = lax.broadcasted_iota(jnp.int32, logits.shape, 1)
        neg = -jnp.inf
        g_log = jnp.where(_group_lanes(lane), logits, neg)
        g_max = jnp.max(g_log, axis=-1, keepdims=True)
        p_group = 1.0 / jnp.sum(jnp.exp(g_log - g_max), axis=-1, keepdims=True)
        first = group * EXPERTS_PER_GROUP
        e_log = jnp.where((lane >= first) & (lane < first + EXPERTS_PER_GROUP), logits, neg)
        v1 = jnp.max(e_log, axis=-1, keepdims=True)
        i1 = jnp.min(jnp.where(e_log == v1, lane, ROUTER_NO_LANE), axis=-1, keepdims=True)
        e_log2 = jnp.where(lane == i1, neg, e_log)
        v2 = jnp.max(e_log2, axis=-1, keepdims=True)
        i2 = jnp.min(jnp.where(e_log2 == v2, lane, ROUTER_NO_LANE), axis=-1, keepdims=True)
        e2 = jnp.exp(v2 - v1)
        w1 = p_group / (1.0 + e2)
        w2 = p_group * e2 / (1.0 + e2)
        combine = jnp.where(lane == i1, w1, 0.0) + jnp.where(lane == i2, w2, 0.0)

        xb = x.astype(BF16)
        hidden = []
        for e in range(EXPERTS_PER_GROUP):
            w_tok = jnp.sum(jnp.where(lane == first + e, combine, 0.0), axis=-1, keepdims=True)
            hg = _dot(xb, wg_ref[e])
            hu = _dot(xb, wu_ref[e])
            hidden.append((hg * jax.nn.sigmoid(hg) * hu * w_tok).astype(BF16))
        y = _dot(jnp.concatenate(hidden, axis=1), wd_ref[...])

        @pl.when(i >= 1)
        def _():
            wait_scatter(tile_valid_ref[jnp.maximum(i - 1, 0)])

        ybuf[...] = y
        n_valid = tile_valid_ref[i]
        start_scatter(n_valid)

        @pl.when(i == n_tiles - 1)
        def _():
            wait_scatter(n_valid)


def _experts(u2, gidx, w_r, b_r, w_g, w_u, w_d):
    n, d = u2.shape
    tm = MOE_ROWS
    d_e = w_g.shape[-1]
    tile_group, tile_valid, n_tiles, slot_token = _route_plan(gidx, tm)
    max_tiles = slot_token.shape[0]
    w_g = w_g.reshape(N_GROUPS, EXPERTS_PER_GROUP, d, d_e)
    w_u = w_u.reshape(N_GROUPS, EXPERTS_PER_GROUP, d, d_e)
    w_d = w_d.reshape(N_GROUPS, EXPERTS_PER_GROUP * d_e, d)
    grid_spec = pltpu.PrefetchScalarGridSpec(
        num_scalar_prefetch=3,
        grid=(max_tiles,),
        in_specs=[pl.BlockSpec((None, 1, tm), lambda i, tg, tv, nt: (i, 0, 0), memory_space=pltpu.SMEM),
                  pl.BlockSpec((None, 1, tm), lambda i, tg, tv, nt: (jnp.minimum(i + 1, max_tiles - 1), 0, 0),
                               memory_space=pltpu.SMEM),
                  pl.BlockSpec(memory_space=pl.ANY),
                  pl.BlockSpec(w_r.shape, lambda i, tg, tv, nt: (0, 0)),
                  pl.BlockSpec(b_r.shape, lambda i, tg, tv, nt: (0, 0)),
                  pl.BlockSpec((None, EXPERTS_PER_GROUP, d, d_e), lambda i, tg, tv, nt: (tg[i], 0, 0, 0)),
                  pl.BlockSpec((None, EXPERTS_PER_GROUP, d, d_e), lambda i, tg, tv, nt: (tg[i], 0, 0, 0)),
                  pl.BlockSpec((None, EXPERTS_PER_GROUP * d_e, d), lambda i, tg, tv, nt: (tg[i], 0, 0))],
        out_specs=pl.BlockSpec(memory_space=pl.ANY),
        scratch_shapes=[pltpu.VMEM((2, tm, d), F32),
                        pltpu.VMEM((tm, d), F32),
                        pltpu.SemaphoreType.DMA((2,)),
                        pltpu.SemaphoreType.DMA(())],
    )
    return pl.pallas_call(
        functools.partial(_experts_kernel, tm=tm),
        grid_spec=grid_spec,
        out_shape=jax.ShapeDtypeStruct((n, d), F32),
        compiler_params=_cparams(("arbitrary",)),
        name="experts",
    )(tile_group, tile_valid, n_tiles, slot_token, slot_token, u2, w_r, b_r, w_g, w_u, w_d)


def _ln2_kernel(x1_ref, y_ref, gate2_ref, ln_g_ref, ln_b_ref, o_ref):
    v = DEEPNORM_ALPHA * x1_ref[...] + gate2_ref[...] * y_ref[...]
    o_ref[...] = _layer_norm(v, ln_g_ref[...], ln_b_ref[...])


def _ln2(x1, y, mod4, ln_g, ln_b, seq):
    n, d = x1.shape
    tm = min(LN2_ROWS, seq)
    tiles_per_seq = seq // tm
    rows = pl.BlockSpec((tm, d), lambda i: (i, 0))
    vec = pl.BlockSpec((1, d), lambda i: (0, 0))
    return pl.pallas_call(
        _ln2_kernel,
        grid=(n // tm,),
        in_specs=[rows, rows, pl.BlockSpec((None, None, 1, d), lambda i: (i // tiles_per_seq, 5, 0, 0)), vec, vec],
        out_specs=rows,
        out_shape=jax.ShapeDtypeStruct((n, d), F32),
        compiler_params=_cparams(("arbitrary",)),
        name="ln2",
    )(x1, y, mod4, ln_g, ln_b)


def kernel(x, c, positions, w_ada, b_ada, w_in, mla_q_norm_g, w_q_up, mla_kv_norm_g, w_kv_up, w_branch_sb, w_branch_mla, w_out, ln1_g, ln1_b, w_router_group, b_router_group, w_router_expert, b_router_expert, w_exp_gate, w_exp_up, w_exp_down, ln2_g, ln2_b):
    bsz, seq, d = x.shape
    n = bsz * seq
    invf = _rope_inv_freq_tile()
    posf = positions.astype(F32).reshape(n, 1)
    tri = _suffix_sum_matrix()
    for l in range(w_ada.shape[0]):
        x2 = x.reshape(n, d)
        mod4 = _adaln_mod(c, w_ada[l], b_ada[l]).reshape(bsz, N_MOD, 1, d)
        weights = _prep_in_weights(w_in[l], w_q_up[l], w_kv_up[l])
        sb, qm, km, vm, gates = _in_proj(x2, mod4, posf, invf, weights,
                                         mla_q_norm_g[l].reshape(1, -1), mla_kv_norm_g[l].reshape(1, -1), seq)
        o_sb = _sb_attention(sb.reshape(bsz, seq, -1), tri).reshape(n, SB_WIDTH)
        o_mla = _mla_attention(qm.reshape(bsz, seq, -1), km.reshape(bsz, seq, -1),
                               vm.reshape(bsz, seq, -1)).reshape(n, -1)
        w_r = jnp.zeros((d, LANES), F32)
        w_r = w_r.at[:, :N_EXPERTS].set(w_router_expert[l]).at[:, N_EXPERTS:N_EXPERTS + N_GROUPS].set(w_router_group[l])
        b_r = jnp.zeros((1, LANES), F32)
        b_r = b_r.at[0, :N_EXPERTS].set(b_router_expert[l]).at[0, N_EXPERTS:N_EXPERTS + N_GROUPS].set(b_router_group[l])
        w_r = w_r.astype(BF16)
        x1, u2, gidx = _merge(o_sb, o_mla, gates, x2, mod4,
                              w_branch_sb[l].astype(BF16), w_branch_mla[l].astype(BF16), w_out[l].astype(BF16),
                              ln1_g[l].reshape(1, d), ln1_b[l].reshape(1, d), w_r, b_r, seq)
        y = _experts(u2, gidx.reshape(n), w_r, b_r, w_exp_gate[l].astype(BF16), w_exp_up[l].astype(BF16),
                     w_exp_down[l].astype(BF16))
        out = _ln2(x1, y, mod4, ln2_g[l].reshape(1, d), ln2_b[l].reshape(1, d), seq)
        x = out.reshape(bsz, seq, d)
    return x
```
